```python
import jax, jax.numpy as jnp
from jax import lax
import numpy as np

D_MODEL = 1024
BATCH = 16
SEQ = 2048
DEPTH = 2
DEC_BATCH = 32
DEC_SEQ = 4
PAST_LEN = 16384
PAGE_SIZE = 128

N_MIXERS = 2
N_ATTN_LAYERS = (DEPTH + 1) // 2
N_SGU_LAYERS = DEPTH // 2
N_HEADS = 8
HEAD_DIM = D_MODEL // N_HEADS
N_KV_HEADS = 2
ROT_DIM = HEAD_DIM // 4
ROPE_THETA = 500000.0
N_IDX_HEADS = 4
IDX_DIM = 64
IDX_ROT_DIM = IDX_DIM // 4
TOPK_MAX = 256
Q_BLOCK = 128
Q_W = N_HEADS * HEAD_DIM
KV_W = N_KV_HEADS * HEAD_DIM
QI_W = N_IDX_HEADS * IDX_DIM
ATTN_SPLITS = [int(c) for c in np.cumsum([Q_W, KV_W, KV_W, QI_W, IDX_DIM])]
ATTN_IN = ATTN_SPLITS[-1] + N_IDX_HEADS
CHUNK = 128
SGU_WIDTH = D_MODEL
SGU_GROUPS = 4
SGU_GROUP_DIM = SGU_WIDTH // SGU_GROUPS
D_FF = 2816
CONV_W = 3
NORM_EPS = 1e-6
LN_EPS = 1e-5

kernel_name = "dsa_gmlp_convffn_hybrid_step"


def rmsnorm(x, g):
    xf = x.astype(jnp.float32)
    y = xf * lax.rsqrt(jnp.mean(xf * xf, axis=-1, keepdims=True) + NORM_EPS)
    return (y * g.astype(jnp.float32)).astype(x.dtype)


def layernorm(x, g, b):
    xf = x.astype(jnp.float32)
    mu = jnp.mean(xf, axis=-1, keepdims=True)
    xc = xf - mu
    y = xc * lax.rsqrt(jnp.mean(xc * xc, axis=-1, keepdims=True) + LN_EPS)
    return (y * g.astype(jnp.float32) + b.astype(jnp.float32)).astype(x.dtype)


def partial_rope(x, pos, rot_dim):
    half = rot_dim // 2
    freqs = ROPE_THETA ** (-(jnp.arange(half, dtype=jnp.float32) * 2.0) / rot_dim)
    ang = pos.astype(jnp.float32)[:, None] * freqs[None, :]
    cos = jnp.cos(ang)[:, None, :]
    sin = jnp.sin(ang)[:, None, :]
    xf = x.astype(jnp.float32)
    x1 = xf[..., :half]
    x2 = xf[..., half:rot_dim]
    out = jnp.concatenate([x1 * cos - x2 * sin, x1 * sin + x2 * cos, xf[..., rot_dim:]], axis=-1)
    return out.astype(x.dtype)


gather_rows = jax.vmap(lambda rows, ids: rows[ids])


def attn_project(h, w_in, pos):
    B, T, _ = h.shape
    z = h @ w_in
    q, k, v, qi, ki, wi = jnp.split(z, ATTN_SPLITS, axis=-1)
    q = partial_rope(q.reshape(B, T, N_HEADS, HEAD_DIM), pos, ROT_DIM)
    k = partial_rope(k.reshape(B, T, N_KV_HEADS, HEAD_DIM), pos, ROT_DIM)
    v = v.reshape(B, T, N_KV_HEADS, HEAD_DIM)
    qi = partial_rope(qi.reshape(B, T, N_IDX_HEADS, IDX_DIM), pos, IDX_ROT_DIM)
    ki = partial_rope(ki[:, :, None, :], pos, IDX_ROT_DIM)[:, :, 0, :]
    wi = wi.astype(jnp.float32) * ((N_IDX_HEADS ** -0.5) * (IDX_DIM ** -0.5))
    return q, k, v, qi, ki, wi


def index_select(qi, wi, ki_all, q_pos, n_sel):
    s = jnp.einsum('bqhd,bsd->bqhs', qi, ki_all, preferred_element_type=jnp.float32)
    score = jnp.einsum('bqh,bqhs->bqs', wi, jax.nn.relu(s))
    key_pos = jnp.arange(ki_all.shape[1])
    causal = key_pos[None, :] <= q_pos[:, None]
    score = jnp.where(causal[None], score, -jnp.inf)
    _, idx = lax.top_k(score, n_sel)
    valid = idx <= q_pos[None, :, None]
    return idx, valid


def sparse_attend(q, k_sel, v_sel, valid):
    B, Tq = q.shape[:2]
    qg = q.reshape(B, Tq, N_KV_HEADS, N_HEADS // N_KV_HEADS, HEAD_DIM)
    s = jnp.einsum('bqhgd,bqnhd->bqhgn', qg, k_sel, preferred_element_type=jnp.float32) * (HEAD_DIM ** -0.5)
    s = jnp.where(valid[:, :, None, None, :], s, -jnp.inf)
    p = jax.nn.softmax(s, axis=-1)
    o = jnp.einsum('bqhgn,bqnhd->bqhgd', p, v_sel.astype(jnp.float32))
    return o.reshape(B, Tq, Q_W).astype(q.dtype)


def attn_prompt(h, w_in, w_out):
    B, S, _ = h.shape
    pos = jnp.arange(S)
    q, k, v, qi, ki, wi = attn_project(h, w_in, pos)
    n_sel = min(TOPK_MAX, S // 4)
    nb = S // Q_BLOCK

    def blocks(a):
        return a.reshape((B, nb, Q_BLOCK) + a.shape[2:]).swapaxes(0, 1)

    def body(xs):
        qb, qib, wib, pb = xs
        idx, valid = index_select(qib, wib, ki, pb, n_sel)
        return sparse_attend(qb, gather_rows(k, idx), gather_rows(v, idx), valid)

    o = lax.map(body, (blocks(q), blocks(qi), blocks(wi), pos.reshape(nb, Q_BLOCK)))
    o = o.swapaxes(0, 1).reshape(B, S, Q_W)
    return o @ w_out, k, v, ki


def attn_sample(h, cache_k, cache_v, cache_ki, page_table, w_in, w_out):
    B, T, _ = h.shape
    past = page_table.shape[1] * PAGE_SIZE
    pos = past + jnp.arange(T)
    q, k, v, qi, ki, wi = attn_project(h, w_in, pos)
    ki_past = cache_ki[page_table].reshape(B, past, IDX_DIM).astype(ki.dtype)
    ki_all = jnp.concatenate([ki_past, ki], axis=1)
    n_sel = min(TOPK_MAX, (past + T) // 4)
    idx, valid = index_select(qi, wi, ki_all, pos, n_sel)
    in_past = (idx < past)[..., None, None]
    pidx = jnp.minimum(idx, past - 1)
    phys = jax.vmap(lambda pt, p: pt[p])(page_table, pidx // PAGE_SIZE)
    off = pidx % PAGE_SIZE
    nidx = jnp.clip(idx - past, 0, T - 1)
    k_sel = jnp.where(in_past, cache_k[phys, off].astype(k.dtype), gather_rows(k, nidx))
    v_sel = jnp.where(in_past, cache_v[phys, off].astype(v.dtype), gather_rows(v, nidx))
    o = sparse_attend(q, k_sel, v_sel, valid)
    return o @ w_out, k, v, ki


def sgu_layer(h, w_in, ln_g, ln_b, ws, bs, w_out):
    B, T, _ = h.shape
    u, v = jnp.split(h @ w_in, 2, axis=-1)
    v = layernorm(v, ln_g, ln_b)
    pad = (-T) % CHUNK
    up = jnp.pad(u, ((0, 0), (0, pad), (0, 0)))
    vp = jnp.pad(v, ((0, 0), (0, pad), (0, 0)))
    nc = (T + pad) // CHUNK
    mask = jnp.tril(jnp.ones((CHUNK, CHUNK), dtype=bool))
    wsm = jnp.where(mask[None], ws, jnp.zeros_like(ws))
    vc = vp.reshape(B, nc, CHUNK, SGU_GROUPS, SGU_GROUP_DIM)
    s = jnp.einsum('gts,bcsgd->bctgd', wsm, vc) + bs.T[:, :, None]
    y = (up * s.reshape(B, T + pad, SGU_WIDTH))[:, :T]
    return y @ w_out, v


def conv_ffn(h, prev, w_up, conv_w, conv_b, w_down):
    T = h.shape[1]
    up = h @ w_up
    ext = jnp.concatenate([prev.astype(up.dtype), up], axis=1)
    c = conv_b
    for j in range(CONV_W):
        c = c + conv_w[j] * ext[:, j:j + T]
    g, val = jnp.split(c, 2, axis=-1)
    out = (jax.nn.silu(g) * val) @ w_down
    return out, ext[:, ext.shape[1] - (CONV_W - 1):]


def setup_inputs(seed: int = 0) -> dict:
    key = jax.random.key(seed)
    ks = jax.random.split(key, 24)
    f32 = jnp.float32
    n_pages = PAST_LEN // PAGE_SIZE
    in_use = DEC_BATCH * n_pages
    n_pool = in_use + in_use // 4

    def nrm(k, shape, scale):
        return jax.random.normal(k, shape, f32) * scale

    page_table = jax.random.permutation(ks[6], n_pool)[:in_use].reshape(DEC_BATCH, n_pages).astype(jnp.int32)
    return {
        'x_prompt': nrm(ks[0], (BATCH, SEQ, D_MODEL), 1.0),
        'x_sample': nrm(ks[1], (DEC_BATCH, DEC_SEQ, D_MODEL), 1.0),
        'cache_k': nrm(ks[2], (N_ATTN_LAYERS, n_pool, PAGE_SIZE, N_KV_HEADS, HEAD_DIM), 1.0),
        'cache_v': nrm(ks[3], (N_ATTN_LAYERS, n_pool, PAGE_SIZE, N_KV_HEADS, HEAD_DIM), 1.0),
        'cache_idx_k': nrm(ks[4], (N_ATTN_LAYERS, n_pool, PAGE_SIZE, IDX_DIM), 1.0),
        'state_ffn_conv': nrm(ks[5], (DEPTH, DEC_BATCH, CONV_W - 1, 2 * D_FF), 1.0),
        'page_table': page_table,
        'attn_norm': 1.0 + nrm(ks[7], (N_ATTN_LAYERS, D_MODEL), 0.02),
        'w_attn_in': nrm(ks[8], (N_ATTN_LAYERS, D_MODEL, ATTN_IN), D_MODEL ** -0.5),
        'w_attn_out': nrm(ks[9], (N_ATTN_LAYERS, Q_W, D_MODEL), Q_W ** -0.5),
        'sgu_norm': 1.0 + nrm(ks[10], (N_SGU_LAYERS, D_MODEL), 0.02),
        'w_sgu_in': nrm(ks[11], (N_SGU_LAYERS, D_MODEL, 2 * SGU_WIDTH), D_MODEL ** -0.5),
        'sgu_ln_g': 1.0 + nrm(ks[12], (N_SGU_LAYERS, SGU_WIDTH), 0.02),
        'sgu_ln_b': nrm(ks[13], (N_SGU_LAYERS, SGU_WIDTH), 0.02),
        'sgu_ws': nrm(ks[14], (N_SGU_LAYERS, SGU_GROUPS, CHUNK, CHUNK), CHUNK ** -0.5),
        'sgu_bs': 1.0 + nrm(ks[15], (N_SGU_LAYERS, SGU_GROUPS, CHUNK), 0.01),
        'w_sgu_out': nrm(ks[16], (N_SGU_LAYERS, SGU_WIDTH, D_MODEL), SGU_WIDTH ** -0.5),
        'ffn_norm': 1.0 + nrm(ks[17], (DEPTH, D_MODEL), 0.02),
        'w_ffn_up': nrm(ks[18], (DEPTH, D_MODEL, 2 * D_FF), D_MODEL ** -0.5),
        'ffn_conv_w': nrm(ks[19], (DEPTH, CONV_W, 2 * D_FF), CONV_W ** -0.5),
        'ffn_conv_b': nrm(ks[20], (DEPTH, 2 * D_FF), 0.01),
        'w_ffn_down': nrm(ks[21], (DEPTH, D_FF, D_MODEL), D_FF ** -0.5),
        'final_norm': 1.0 + nrm(ks[22], (D_MODEL,), 0.02),
    }


def reference(x_prompt, x_sample, cache_k, cache_v, cache_idx_k, state_ffn_conv, page_table,
              attn_norm, w_attn_in, w_attn_out, sgu_norm, w_sgu_in, sgu_ln_g, sgu_ln_b, sgu_ws, sgu_bs,
              w_sgu_out, ffn_norm, w_ffn_up, ffn_conv_w, ffn_conv_b, w_ffn_down, final_norm):
    hp, hs = x_prompt, x_sample
    kp_l, vp_l, kip_l, ks_l, vs_l, kis_l, sv_l, cp_l, cs_l = [], [], [], [], [], [], [], [], []
    for i in range(DEPTH):
        if i % N_MIXERS == 0:
            a = i // N_MIXERS
            mp, kp, vp, kip = attn_prompt(rmsnorm(hp, attn_norm[a]), w_attn_in[a], w_attn_out[a])
            ms, ksm, vsm, kism = attn_sample(rmsnorm(hs, attn_norm[a]), cache_k[a], cache_v[a], cache_idx_k[a],
                                             page_table, w_attn_in[a], w_attn_out[a])
            kp_l.append(kp); vp_l.append(vp); kip_l.append(kip)
            ks_l.append(ksm); vs_l.append(vsm); kis_l.append(kism)
        else:
            g = i // N_MIXERS
            mp, _ = sgu_layer(rmsnorm(hp, sgu_norm[g]), w_sgu_in[g], sgu_ln_g[g], sgu_ln_b[g], sgu_ws[g], sgu_bs[g], w_sgu_out[g])
            ms, sv = sgu_layer(rmsnorm(hs, sgu_norm[g]), w_sgu_in[g], sgu_ln_g[g], sgu_ln_b[g], sgu_ws[g], sgu_bs[g], w_sgu_out[g])
            sv_l.append(sv)
        hp = hp + mp
        hs = hs + ms
        zeros_prev = jnp.zeros((hp.shape[0], CONV_W - 1, 2 * D_FF), hp.dtype)
        fp, cp = conv_ffn(rmsnorm(hp, ffn_norm[i]), zeros_prev, w_ffn_up[i], ffn_conv_w[i], ffn_conv_b[i], w_ffn_down[i])
        fs, cs = conv_ffn(rmsnorm(hs, ffn_norm[i]), state_ffn_conv[i], w_ffn_up[i], ffn_conv_w[i], ffn_conv_b[i], w_ffn_down[i])
        hp = hp + fp
        hs = hs + fs
        cp_l.append(cp); cs_l.append(cs)
    y_prompt = rmsnorm(hp, final_norm)
    y_sample = rmsnorm(hs, final_norm)
    return (y_prompt, y_sample, jnp.stack(kp_l), jnp.stack(vp_l), jnp.stack(kip_l), jnp.stack(ks_l), jnp.stack(vs_l),
            jnp.stack(kis_l), jnp.stack(sv_l), jnp.stack(cp_l), jnp.stack(cs_l))
```

```python
import functools

import numpy as np
import jax
import jax.numpy as jnp
from jax import lax
from jax.experimental import pallas as pl
from jax.experimental.pallas import tpu as pltpu

F32 = jnp.float32
BF16 = jnp.bfloat16

D_MODEL = 1024
N_HEADS = 8
HEAD_DIM = 128
N_KV_HEADS = 2
HEADS_PER_KV = N_HEADS // N_KV_HEADS
ROT_DIM = 32
ROPE_THETA = 500000.0
N_IDX_HEADS = 4
IDX_DIM = 64
IDX_ROT_DIM = 16
TOPK_MAX = 256
PAGE_SIZE = 128
Q_W = N_HEADS * HEAD_DIM
KV_W = N_KV_HEADS * HEAD_DIM
QI_W = N_IDX_HEADS * IDX_DIM
ATTN_IN = Q_W + 2 * KV_W + QI_W + IDX_DIM + N_IDX_HEADS
ATTN_IN_PAD = 1920
KW_COL = Q_W + 2 * KV_W + QI_W
CHUNK = 128
SGU_GROUPS = 4
SGU_GROUP_DIM = D_MODEL // SGU_GROUPS
D_FF = 2816
CONV_W = 3
NORM_EPS = 1e-6
LN_EPS = 1e-5
ATTN_SCALE = HEAD_DIM ** -0.5
W_IDX_SCALE = (N_IDX_HEADS ** -0.5) * (IDX_DIM ** -0.5)

LANES = 128
SUBLANES = 8
KB = 128
FF_CHUNK = 256
PAGES_PER_STEP = 8
VMEM_LIMIT = 56 * 1024 * 1024
NEG_BIG = -1e30
IDX_BIG = 1e9

_NT = (((1,), (1,)), ((), ()))


def _cparams(n_axes):
    return pltpu.CompilerParams(
        dimension_semantics=("arbitrary",) * n_axes, vmem_limit_bytes=VMEM_LIMIT)


def _rmsnorm(x, g):
    return x * lax.rsqrt(jnp.mean(x * x, axis=-1, keepdims=True) + NORM_EPS) * g


def _rope(u, c, sa, sb, half):
    return u * c + pltpu.roll(u, LANES - half, 1) * sa + pltpu.roll(u, half, 1) * sb


def _attn_in_kernel(x_ref, g_ref, w_ref, ch_ref, sha_ref, shb_ref, ci_ref, sia_ref, sib_ref,
                    ckw_ref, skwa_ref, skwb_ref,
                    q_ref, k_ref, v_ref, kb_ref, vb_ref, qi_ref, kw_ref, kib_ref):
    xn = _rmsnorm(x_ref[...], g_ref[...]).astype(BF16)
    z = jnp.dot(xn, w_ref[...], preferred_element_type=F32)
    ch, sha, shb = ch_ref[...], sha_ref[...], shb_ref[...]
    for h in range(N_HEADS):
        sl = slice(h * HEAD_DIM, (h + 1) * HEAD_DIM)
        q_ref[:, sl] = _rope(z[:, sl], ch, sha, shb, ROT_DIM // 2).astype(BF16)
    for h in range(N_KV_HEADS):
        sl = slice(h * HEAD_DIM, (h + 1) * HEAD_DIM)
        kk = _rope(z[:, Q_W + h * HEAD_DIM:Q_W + (h + 1) * HEAD_DIM], ch, sha, shb, ROT_DIM // 2)
        k_ref[:, sl] = kk
        kb_ref[:, sl] = kk.astype(BF16)
    v = z[:, Q_W + KV_W:Q_W + 2 * KV_W]
    v_ref[...] = v
    vb_ref[...] = v.astype(BF16)
    ci, sia, sib = ci_ref[...], sia_ref[...], sib_ref[...]
    qi0 = Q_W + 2 * KV_W
    for s in range(QI_W // LANES):
        sl = slice(s * LANES, (s + 1) * LANES)
        qi_ref[:, sl] = _rope(z[:, qi0 + s * LANES:qi0 + (s + 1) * LANES], ci, sia, sib,
                              IDX_ROT_DIM // 2).astype(BF16)
    kw = _rope(z[:, KW_COL:KW_COL + LANES], ckw_ref[...], skwa_ref[...], skwb_ref[...],
               IDX_ROT_DIM // 2)
    kw_ref[...] = kw
    kib_ref[...] = kw[:, :IDX_DIM].astype(BF16)


def _rope_tables(pos, rot, period, tail_scale=None):
    half = rot // 2
    freqs = ROPE_THETA ** (-(jnp.arange(half, dtype=F32) * 2.0) / rot)
    ang = pos.astype(F32)[:, None] * freqs[None, :]
    cos, sin = jnp.cos(ang), jnp.sin(ang)
    lane = np.arange(LANES) % period
    col = lane % half
    c = jnp.where(lane < rot, cos[:, col], 1.0)
    sa = jnp.where(lane < half, -sin[:, col], 0.0)
    sb = jnp.where((lane >= half) & (lane < rot), sin[:, col], 0.0)
    if tail_scale is not None:
        glane = np.arange(LANES)
        c = jnp.where(glane < period, c, 1.0)
        c = jnp.where((glane >= IDX_DIM) & (glane < IDX_DIM + N_IDX_HEADS), tail_scale, c)
        sa = jnp.where(glane < period, sa, 0.0)
        sb = jnp.where(glane < period, sb, 0.0)
    return c.astype(F32), sa.astype(F32), sb.astype(F32)


def _attn_in(x, pos, g, w_pad, tm):
    B, T, _ = x.shape
    ch, sha, shb = _rope_tables(pos, ROT_DIM, HEAD_DIM)
    ci, sia, sib = _rope_tables(pos, IDX_ROT_DIM, IDX_DIM)
    ckw, skwa, skwb = _rope_tables(pos, IDX_ROT_DIM, IDX_DIM, tail_scale=W_IDX_SCALE)
    row = lambda b, t: (b, t, 0)
    tab = lambda b, t: (t, 0)
    const = lambda b, t: (0, 0)
    blk = lambda w: pl.BlockSpec((None, tm, w), row)
    tspec = pl.BlockSpec((tm, LANES), tab)
    outs = pl.pallas_call(
        _attn_in_kernel,
        grid=(B, T // tm),
        in_specs=[blk(D_MODEL), pl.BlockSpec((1, D_MODEL), const),
                  pl.BlockSpec((D_MODEL, ATTN_IN_PAD), const),
                  tspec, tspec, tspec, tspec, tspec, tspec, tspec, tspec, tspec],
        out_specs=[blk(Q_W), blk(KV_W), blk(KV_W), blk(KV_W), blk(KV_W), blk(QI_W),
                   blk(LANES), blk(IDX_DIM)],
        out_shape=[jax.ShapeDtypeStruct((B, T, Q_W), BF16),
                   jax.ShapeDtypeStruct((B, T, KV_W), F32),
                   jax.ShapeDtypeStruct((B, T, KV_W), F32),
                   jax.ShapeDtypeStruct((B, T, KV_W), BF16),
                   jax.ShapeDtypeStruct((B, T, KV_W), BF16),
                   jax.ShapeDtypeStruct((B, T, QI_W), BF16),
                   jax.ShapeDtypeStruct((B, T, LANES), F32),
                   jax.ShapeDtypeStruct((B, T, IDX_DIM), BF16)],
        compiler_params=_cparams(2),
        name="attn_in",
    )(x, g, w_pad, ch, sha, shb, ci, sia, sib, ckw, skwa, skwb)
    return outs


def _search_threshold(scan, a0, b0, n0, kf):
    def step(st, interp):
        a, b, ca, cb = st
        mid = 0.5 * a + 0.5 * b
        frac = (kf - 0.5 - cb) / jnp.maximum(ca - cb, 1.0)
        p = jnp.where(interp, b - (b - a) * frac, mid)
        p = jnp.minimum(jnp.maximum(p, a), b)
        p = jnp.where(p > a, p, b)
        cnt, lowmax, highmin = scan(p)
        act = a < b
        up = act & (cnt >= kf)
        dn = act & (cnt < kf)
        return (jnp.where(up, highmin, a), jnp.where(dn, lowmax, b),
                jnp.where(up, cnt, ca), jnp.where(dn, cnt, cb))

    def cond(carry):
        return carry[0] > 0

    def body(carry):
        st = carry[1:]
        st = step(st, True)
        st = step(st, False)
        st = step(st, False)
        open_ = jnp.max(jnp.where(st[0] < st[1], 1, 0).astype(jnp.int32))
        return (open_,) + st

    b0 = jnp.where(n0 <= kf, a0, b0)
    open0 = jnp.max(jnp.where(a0 < b0, 1, 0).astype(jnp.int32))
    out = lax.while_loop(cond, body, (open0, a0, b0, n0, jnp.zeros_like(a0)))
    return out[1]


def _colreduce8(x, op):
    return op(x.reshape(KB // SUBLANES, SUBLANES, LANES), axis=0)


def _attn_prompt_kernel(q_ref, kb_ref, vb_ref, qi_ref, kib_ref, wi_ref, o_ref,
                        sc_ref, e_ref, bias_ref, satt_ref, vext_ref, *, n_sel, idx_bits):
    i = pl.program_id(1)
    nch = i + 1
    kf = float(n_sel)

    @pl.when(i == 0)
    def _():
        for g in range(N_KV_HEADS):
            vext_ref[g, :, :HEAD_DIM] = vb_ref[:, g * HEAD_DIM:(g + 1) * HEAD_DIM]
            vext_ref[g, :, HEAD_DIM:] = jnp.ones((vext_ref.shape[1], HEAD_DIM), BF16)

    def chunk(j):
        return pl.ds(pl.multiple_of(j * KB, KB), KB)

    w = wi_ref[...]
    qi = qi_ref[...]
    qih = [qi[:, h * IDX_DIM:(h + 1) * IDX_DIM] for h in range(N_IDX_HEADS)]
    krow = lax.broadcasted_iota(jnp.int32, (KB, KB), 0)
    qcol = lax.broadcasted_iota(jnp.int32, (KB, KB), 1)
    kmq = krow - qcol
    inf = jnp.float32(jnp.inf)

    def p1(j, carry):
        amin, amax = carry
        kic = kib_ref[chunk(j), :]
        s = None
        for h in range(N_IDX_HEADS):
            sh = lax.dot_general(kic, qih[h], _NT, preferred_element_type=F32)
            t = w[h:h + 1, :] * jnp.maximum(sh, 0.0)
            s = t if s is None else s + t
        causal = kmq <= (i - j) * KB
        sc_ref[chunk(j), :] = jnp.where(causal, s, -inf)
        amin = jnp.minimum(amin, _colreduce8(jnp.where(causal, s, inf), jnp.min))
        amax = jnp.maximum(amax, _colreduce8(jnp.where(causal, s, -inf), jnp.max))
        return amin, amax

    amin8, amax8 = lax.fori_loop(
        0, nch, p1, (jnp.full((SUBLANES, LANES), inf, F32), jnp.full((SUBLANES, LANES), -inf, F32)))
    a0 = jnp.min(amin8, axis=0, keepdims=True)
    b0 = jnp.max(amax8, axis=0, keepdims=True)
    lane = lax.broadcasted_iota(jnp.int32, (1, LANES), 1)
    n0 = (i * KB + lane + 1).astype(F32)

    def scan(p):
        def body(j, carry):
            cnt, lo, hi = carry
            x = sc_ref[chunk(j), :]
            ge = x >= p
            cnt = cnt + _colreduce8(jnp.where(ge, 1.0, 0.0), jnp.sum)
            lo = jnp.maximum(lo, _colreduce8(jnp.where(ge, -inf, x), jnp.max))
            hi = jnp.minimum(hi, _colreduce8(jnp.where(ge, x, inf), jnp.min))
            return cnt, lo, hi
        z8 = jnp.zeros((SUBLANES, LANES), F32)
        cnt, lo, hi = lax.fori_loop(0, nch, body, (z8, z8 - inf, z8 + inf))
        return (jnp.sum(cnt, axis=0, keepdims=True), jnp.max(lo, axis=0, keepdims=True),
                jnp.min(hi, axis=0, keepdims=True))

    thr = _search_threshold(scan, a0, b0, n0, kf)

    def p3(j, cgt):
        x = sc_ref[chunk(j), :]
        kidx = (krow + j * KB).astype(F32)
        e_ref[chunk(j), :] = jnp.where(x == thr, kidx, IDX_BIG)
        return cgt + _colreduce8(jnp.where(x > thr, 1.0, 0.0), jnp.sum)

    cgt = jnp.sum(lax.fori_loop(0, nch, p3, jnp.zeros((SUBLANES, LANES), F32)),
                  axis=0, keepdims=True)
    need = kf - cgt

    xsel = jnp.zeros((1, LANES), F32)
    for bit in reversed(range(idx_bits)):
        cand = xsel + float(1 << bit)

        def pcount(j, acc, cand=cand):
            return acc + _colreduce8(jnp.where(e_ref[chunk(j), :] < cand, 1.0, 0.0), jnp.sum)

        below = jnp.sum(lax.fori_loop(0, nch, pcount, jnp.zeros((SUBLANES, LANES), F32)),
                        axis=0, keepdims=True)
        xsel = jnp.where(below < need, cand, xsel)

    def p3b(j, c):
        x = sc_ref[chunk(j), :]
        sel_bias = jnp.where(x > thr, 0.0, jnp.where(e_ref[chunk(j), :] <= xsel, 0.0, -inf))
        bias_ref[:, chunk(j)] = sel_bias.T
        return c

    lax.fori_loop(0, nch, p3b, 0)

    for g in range(N_KV_HEADS):
        qg = jnp.concatenate(
            [q_ref[:, (g * HEADS_PER_KV + r) * HEAD_DIM:(g * HEADS_PER_KV + r + 1) * HEAD_DIM]
             for r in range(HEADS_PER_KV)], axis=0)

        def pa(j, macc, g=g, qg=qg):
            kc = kb_ref[chunk(j), g * HEAD_DIM:(g + 1) * HEAD_DIM]
            s = lax.dot_general(qg, kc, _NT, preferred_element_type=F32) * ATTN_SCALE
            bb = bias_ref[:, chunk(j)]
            s = s + jnp.concatenate([bb] * HEADS_PER_KV, axis=0)
            satt_ref[:, chunk(j)] = s
            return jnp.maximum(macc, s)

        macc = lax.fori_loop(0, nch, pa, jnp.full((HEADS_PER_KV * KB, KB), -inf, F32))
        m = jnp.max(macc, axis=1, keepdims=True)

        def pb(j, acc, g=g, m=m):
            p = jnp.exp(satt_ref[:, chunk(j)] - m).astype(BF16)
            return acc + jnp.dot(p, vext_ref[g, chunk(j), :], preferred_element_type=F32)

        acc = lax.fori_loop(0, nch, pb, jnp.zeros((HEADS_PER_KV * KB, 2 * HEAD_DIM), F32))
        o = acc[:, :HEAD_DIM] / acc[:, HEAD_DIM:]
        for r in range(HEADS_PER_KV):
            h = g * HEADS_PER_KV + r
            o_ref[:, h * HEAD_DIM:(h + 1) * HEAD_DIM] = o[r * KB:(r + 1) * KB].astype(BF16)


def _attn_prompt(q, kb, vb, qi, kib, wi_t):
    B, S, _ = q.shape
    n_sel = min(TOPK_MAX, S // 4)
    idx_bits = int(np.ceil(np.log2(S)))
    qblk = lambda w: pl.BlockSpec((None, KB, w), lambda b, i: (b, i, 0))
    full = lambda w: pl.BlockSpec((None, S, w), lambda b, i: (b, 0, 0))
    return pl.pallas_call(
        functools.partial(_attn_prompt_kernel, n_sel=n_sel, idx_bits=idx_bits),
        grid=(B, S // KB),
        in_specs=[qblk(Q_W), full(KV_W), full(KV_W), qblk(QI_W), full(IDX_DIM),
                  pl.BlockSpec((None, N_IDX_HEADS, KB), lambda b, i: (b, 0, i))],
        out_specs=qblk(Q_W),
        out_shape=jax.ShapeDtypeStruct((B, S, Q_W), BF16),
        scratch_shapes=[pltpu.VMEM((S, KB), F32), pltpu.VMEM((S, KB), F32),
                        pltpu.VMEM((KB, S), F32), pltpu.VMEM((HEADS_PER_KV * KB, S), F32),
                        pltpu.VMEM((N_KV_HEADS, S, 2 * HEAD_DIM), BF16)],
        compiler_params=_cparams(2),
        name="attn_prompt",
    )(q, kb, vb, qi, kib, wi_t)


def _idx_scores(qi, wcol, keys):
    s = lax.dot_general(qi, keys, _NT, preferred_element_type=F32)
    r = jnp.maximum(s, 0.0) * wcol
    out = r[0:SUBLANES]
    for h in range(1, N_IDX_HEADS):
        out = out + r[h * SUBLANES:(h + 1) * SUBLANES]
    return out


def _sel_sample_kernel(pt_ref, qi_ref, w_ref, *refs, n_sel, past, idx_bits):
    ki_refs = refs[:PAGES_PER_STEP]
    kinew_ref, sc_ref, thr_ref, xsel_ref = refs[PAGES_PER_STEP:]
    j = pl.program_id(1)
    qi = qi_ref[...]
    wcol = w_ref[...]
    for p in range(PAGES_PER_STEP):
        col = pl.multiple_of((j * PAGES_PER_STEP + p) * PAGE_SIZE, PAGE_SIZE)
        sc_ref[:, pl.ds(col, PAGE_SIZE)] = _idx_scores(qi, wcol, ki_refs[p][...].astype(BF16))

    @pl.when(j == pl.num_programs(1) - 1)
    def _():
        inf = jnp.float32(jnp.inf)
        kf = float(n_sel)
        row = lax.broadcasted_iota(jnp.int32, (SUBLANES, PAGE_SIZE), 0)
        lane = lax.broadcasted_iota(jnp.int32, (SUBLANES, PAGE_SIZE), 1)
        snew = _idx_scores(qi, wcol, kinew_ref[...])
        sc_ref[:, past:past + PAGE_SIZE] = jnp.where(lane <= row, snew, -inf)

        x = sc_ref[...]
        fin = x > -inf
        a0 = jnp.min(jnp.where(fin, x, inf), axis=1, keepdims=True)
        b0 = jnp.max(x, axis=1, keepdims=True)
        n0 = jnp.sum(jnp.where(fin, 1.0, 0.0), axis=1, keepdims=True)

        def scan(p):
            ge = x >= p
            return (jnp.sum(jnp.where(ge, 1.0, 0.0), axis=1, keepdims=True),
                    jnp.max(jnp.where(ge, -inf, x), axis=1, keepdims=True),
                    jnp.min(jnp.where(ge, x, inf), axis=1, keepdims=True))

        thr = _search_threshold(scan, a0, b0, n0, kf)
        cidx = lax.broadcasted_iota(jnp.int32, x.shape, 1).astype(F32)
        e = jnp.where(x == thr, cidx, IDX_BIG)
        need = kf - jnp.sum(jnp.where(x > thr, 1.0, 0.0), axis=1, keepdims=True)
        xsel = jnp.zeros((SUBLANES, 1), F32)
        for bit in reversed(range(idx_bits)):
            cand = xsel + float(1 << bit)
            below = jnp.sum(jnp.where(e < cand, 1.0, 0.0), axis=1, keepdims=True)
            xsel = jnp.where(below < need, cand, xsel)
        thr_ref[...] = jnp.broadcast_to(thr, thr_ref.shape)
        xsel_ref[...] = jnp.broadcast_to(xsel, xsel_ref.shape)


def _sel_sample(page_table, qi_pad, wcol, cache_ki, kinew, n_sel):
    B, n_pages = page_table.shape
    past = n_pages * PAGE_SIZE
    L = past + PAGE_SIZE
    steps = n_pages // PAGES_PER_STEP
    page = lambda p: pl.BlockSpec((None, PAGE_SIZE, IDX_DIM),
                                  lambda b, j, pt: (pt[b, j * PAGES_PER_STEP + p], 0, 0))
    per_b = lambda r, w: pl.BlockSpec((None, r, w), lambda b, j, pt: (b, 0, 0))
    return pl.pallas_call(
        functools.partial(_sel_sample_kernel, n_sel=n_sel, past=past,
                          idx_bits=int(np.ceil(np.log2(L)))),
        grid_spec=pltpu.PrefetchScalarGridSpec(
            num_scalar_prefetch=1,
            grid=(B, steps),
            in_specs=[per_b(N_IDX_HEADS * SUBLANES, IDX_DIM), per_b(N_IDX_HEADS * SUBLANES, LANES)]
                     + [page(p) for p in range(PAGES_PER_STEP)]
                     + [per_b(PAGE_SIZE, IDX_DIM)],
            out_specs=[per_b(SUBLANES, L), per_b(SUBLANES, LANES), per_b(SUBLANES, LANES)],
        ),
        out_shape=[jax.ShapeDtypeStruct((B, SUBLANES, L), F32),
                   jax.ShapeDtypeStruct((B, SUBLANES, LANES), F32),
                   jax.ShapeDtypeStruct((B, SUBLANES, LANES), F32)],
        compiler_params=_cparams(2),
        name="sel_sample",
    )(page_table, qi_pad, wcol, *([cache_ki] * PAGES_PER_STEP), kinew)


def _attn_sample_kernel(pt_ref, q_ref, sc_ref, thr_ref, xsel_ref, *refs, past):
    k_refs = refs[:PAGES_PER_STEP]
    v_refs = refs[PAGES_PER_STEP:2 * PAGES_PER_STEP]
    knew_ref, vnew_ref, o_ref, m_ref, l_ref, acc_ref = refs[2 * PAGES_PER_STEP:]
    j = pl.program_id(1)
    nrow = HEADS_PER_KV * SUBLANES

    @pl.when(j == 0)
    def _():
        m_ref[...] = jnp.full(m_ref.shape, NEG_BIG, F32)
        l_ref[...] = jnp.zeros(l_ref.shape, F32)
        acc_ref[...] = jnp.zeros(acc_ref.shape, F32)

    thr = thr_ref[...]
    xsel = xsel_ref[...]
    lane = lax.broadcasted_iota(jnp.int32, (SUBLANES, PAGE_SIZE), 1)

    def selected(col):
        x = sc_ref[:, pl.ds(col, PAGE_SIZE)]
        cidx = (lane + col).astype(F32)
        sel = jnp.where(x > thr, 1.0, jnp.where(x == thr, jnp.where(cidx <= xsel, 1.0, 0.0), 0.0))
        return jnp.concatenate([sel] * HEADS_PER_KV, axis=0) > 0.5

    def update(g, s_list, v_list):
        s = jnp.concatenate(s_list, axis=1)
        m_old = m_ref[g]
        m_new = jnp.maximum(m_old, jnp.max(s, axis=1, keepdims=True))
        alpha = jnp.exp(m_old - m_new)
        p = jnp.exp(s - m_new[:, :1])
        l_ref[g] = alpha * l_ref[g] + jnp.sum(p, axis=1, keepdims=True)
        pb = p.astype(BF16)
        acc = alpha * acc_ref[g]
        for n, vv in enumerate(v_list):
            acc = acc + jnp.dot(pb[:, n * PAGE_SIZE:(n + 1) * PAGE_SIZE], vv,
                                preferred_element_type=F32)
        acc_ref[g] = acc
        m_ref[g] = m_new

    def scores(g, kk, col):
        s = lax.dot_general(q_ref[g], kk, _NT, preferred_element_type=F32) * ATTN_SCALE
        return jnp.where(selected(col), s, NEG_BIG)

    for g in range(N_KV_HEADS):
        hs = slice(g * HEAD_DIM, (g + 1) * HEAD_DIM)
        s_list, v_list = [], []
        for p in range(PAGES_PER_STEP):
            col = pl.multiple_of((j * PAGES_PER_STEP + p) * PAGE_SIZE, PAGE_SIZE)
            s_list.append(scores(g, k_refs[p][:, hs].astype(BF16), col))
            v_list.append(v_refs[p][:, hs].astype(BF16))
        update(g, s_list, v_list)

    @pl.when(j == pl.num_programs(1) - 1)
    def _():
        for g in range(N_KV_HEADS):
            hs = slice(g * HEAD_DIM, (g + 1) * HEAD_DIM)
            update(g, [scores(g, knew_ref[:, hs], past)], [vnew_ref[:, hs]])
            o_ref[g] = acc_ref[g] / l_ref[g]


def _attn_sample(page_table, q_pad, sc, thr, xsel, cache_k, cache_v, knew, vnew):
    B, n_pages = page_table.shape
    past = n_pages * PAGE_SIZE
    steps = n_pages // PAGES_PER_STEP
    nrow = HEADS_PER_KV * SUBLANES
    page = lambda p: pl.BlockSpec((None, PAGE_SIZE, KV_W),
                                  lambda b, j, pt: (pt[b, j * PAGES_PER_STEP + p], 0, 0))
    per_b = lambda *s: pl.BlockSpec((None,) + s, lambda b, j, pt: (b,) + (0,) * len(s))
    return pl.pallas_call(
        functools.partial(_attn_sample_kernel, past=past),
        grid_spec=pltpu.PrefetchScalarGridSpec(
            num_scalar_prefetch=1,
            grid=(B, steps),
            in_specs=[per_b(N_KV_HEADS, nrow, HEAD_DIM), per_b(SUBLANES, past + PAGE_SIZE),
                      per_b(SUBLANES, LANES), per_b(SUBLANES, LANES)]
                     + [page(p) for p in range(PAGES_PER_STEP)] * 2
                     + [per_b(PAGE_SIZE, KV_W), per_b(PAGE_SIZE, KV_W)],
            out_specs=per_b(N_KV_HEADS, nrow, HEAD_DIM),
            scratch_shapes=[pltpu.VMEM((N_KV_HEADS, nrow, LANES), F32),
                            pltpu.VMEM((N_KV_HEADS, nrow, LANES), F32),
                            pltpu.VMEM((N_KV_HEADS, nrow, HEAD_DIM), F32)],
        ),
        out_shape=jax.ShapeDtypeStruct((B, N_KV_HEADS, nrow, HEAD_DIM), F32),
        compiler_params=_cparams(2),
        name="attn_sample",
    )(page_table, q_pad, sc, thr, xsel, *([cache_k] * PAGES_PER_STEP), *([cache_v] * PAGES_PER_STEP),
      knew, vnew)


def _ffn_kernel(*refs, sample, final):
    it = iter(refs)
    h_ref, o_ref, wout_ref, g_ref, wup_ref, cw_ref, cb_ref, wdown_ref = (next(it) for _ in range(8))
    gfin_ref = next(it) if final else None
    st1_ref, st2_ref = (next(it), next(it)) if sample else (None, None)
    hout_ref, cst_ref = next(it), next(it)
    carry_ref = None if sample else next(it)

    tm = h_ref.shape[0]
    h1 = h_ref[...] + jnp.dot(o_ref[...], wout_ref[...], preferred_element_type=F32)
    xn = _rmsnorm(h1, g_ref[...]).astype(BF16)

    if sample:
        t4 = lax.broadcasted_iota(jnp.int32, (tm, FF_CHUNK), 0) % 4
    else:
        @pl.when(pl.program_id(1) == 0)
        def _():
            carry_ref[...] = jnp.zeros(carry_ref.shape, F32)
        row8 = lax.broadcasted_iota(jnp.int32, (SUBLANES, FF_CHUNK), 0)

    def conv(cols):
        u = jnp.dot(xn, wup_ref[:, cols], preferred_element_type=F32)
        r1 = pltpu.roll(u, 1, 0)
        r2 = pltpu.roll(u, 2, 0)
        if sample:
            s1 = jnp.where(t4 == 0, st1_ref[:, cols], r1)
            s2 = jnp.where(t4 == 0, st2_ref[:, cols], jnp.where(t4 == 1, st1_ref[:, cols], r2))
            cst_ref[:, cols] = u
        else:
            prev = carry_ref[:, cols]
            top1 = jnp.where(row8 < 1, pltpu.roll(prev, 1, 0), r1[:SUBLANES])
            top2 = jnp.where(row8 < 2, pltpu.roll(prev, 2, 0), r2[:SUBLANES])
            s1 = jnp.concatenate([top1, r1[SUBLANES:]], axis=0)
            s2 = jnp.concatenate([top2, r2[SUBLANES:]], axis=0)
            carry_ref[:, cols] = u[tm - SUBLANES:]
            cst_ref[:, cols] = u[tm - SUBLANES:]
        cw = cw_ref[:, cols]
        return cb_ref[:, cols] + cw[0:1] * s2 + cw[1:2] * s1 + cw[2:3] * u

    acc = jnp.zeros((tm, D_MODEL), F32)
    for c in range(D_FF // FF_CHUNK):
        gate = conv(slice(c * FF_CHUNK, (c + 1) * FF_CHUNK))
        val = conv(slice(D_FF + c * FF_CHUNK, D_FF + (c + 1) * FF_CHUNK))
        act = (gate * (1.0 / (1.0 + jnp.exp(-gate))) * val).astype(BF16)
        acc = acc + jnp.dot(act, wdown_ref[c * FF_CHUNK:(c + 1) * FF_CHUNK, :],
                            preferred_element_type=F32)
    out = h1 + acc
    if final:
        out = _rmsnorm(out, gfin_ref[...])
    hout_ref[...] = out


def _ffn(h, o, wout, g, wup, cw, cb, wdown, gfin, state, tm):
    B, T, _ = h.shape
    sample = state is not None
    final = gfin is not None
    row = lambda b, t: (b, t, 0)
    const = lambda b, t: (0, 0)
    w = lambda a: pl.BlockSpec(a.shape, const, pipeline_mode=pl.Buffered(1))
    operands = [h, o, wout, g, wup, cw, cb, wdown]
    in_specs = [pl.BlockSpec((None, tm, D_MODEL), row), pl.BlockSpec((None, tm, D_MODEL), row),
                w(wout), w(g), w(wup), w(cw), w(cb), w(wdown)]
    if final:
        operands.append(gfin)
        in_specs.append(w(gfin))
    if sample:
        operands += [state[0], state[1]]
        in_specs += [w(state[0]), w(state[1])]
        cst_shape = jax.ShapeDtypeStruct((B, T, 2 * D_FF), F32)
        cst_spec = pl.BlockSpec((None, tm, 2 * D_FF), row)
        scratch = []
    else:
        cst_shape = jax.ShapeDtypeStruct((B, SUBLANES, 2 * D_FF), F32)
        cst_spec = pl.BlockSpec((None, SUBLANES, 2 * D_FF), lambda b, t: (b, 0, 0))
        scratch = [pltpu.VMEM((SUBLANES, 2 * D_FF), F32)]
    return pl.pallas_call(
        functools.partial(_ffn_kernel, sample=sample, final=final),
        grid=(B, T // tm),
        in_specs=in_specs,
        out_specs=[pl.BlockSpec((None, tm, D_MODEL), row), cst_spec],
        out_shape=[jax.ShapeDtypeStruct((B, T, D_MODEL), F32), cst_shape],
        scratch_shapes=scratch,
        compiler_params=_cparams(2),
        name="ffn_sample" if sample else "ffn_prompt",
    )(*operands)


def _sgu_kernel(h_ref, g_ref, win_ref, lng_ref, lnb_ref, ws_ref, bias_ref, *out_refs, emit_v):
    y_ref = out_refs[0]
    tm = h_ref.shape[0]
    xn = _rmsnorm(h_ref[...], g_ref[...]).astype(BF16)
    uv = jnp.dot(xn, win_ref[...], preferred_element_type=F32)
    u = uv[:, :D_MODEL]
    v = uv[:, D_MODEL:]
    xc = v - jnp.mean(v, axis=-1, keepdims=True)
    vn = xc * lax.rsqrt(jnp.mean(xc * xc, axis=-1, keepdims=True) + LN_EPS) * lng_ref[...] + lnb_ref[...]
    if emit_v:
        out_refs[1][...] = vn
    vnb = vn.astype(BF16)
    tri = (lax.broadcasted_iota(jnp.int32, (CHUNK, CHUNK), 0)
           >= lax.broadcasted_iota(jnp.int32, (CHUNK, CHUNK), 1))
    for g in range(SGU_GROUPS):
        wg = jnp.where(tri, ws_ref[g], 0.0).astype(BF16)
        cols = slice(g * SGU_GROUP_DIM, (g + 1) * SGU_GROUP_DIM)
        for c in range(tm // CHUNK):
            rows = slice(c * CHUNK, (c + 1) * CHUNK)
            s = jnp.dot(wg, vnb[rows, cols], preferred_element_type=F32) + bias_ref[:, cols]
            y_ref[rows, cols] = (u[rows, cols] * s).astype(BF16)


def _sgu_bias(bs):
    return jnp.repeat(bs.T, SGU_GROUP_DIM, axis=1)


def _sgu_sample_weights(ws, bs, B, T):
    corner = ws[:, :T, :T]
    ws_s = jnp.einsum('bc,gts->gbtcs', jnp.eye(B, dtype=ws.dtype), corner)
    return ws_s.reshape(SGU_GROUPS, B * T, B * T), jnp.tile(_sgu_bias(bs)[:T], (B, 1))


def _sgu(h, g, win, lng, lnb, ws, bias, tm, emit_v):
    B, T, _ = h.shape
    row = lambda b, t: (b, t, 0)
    w = lambda a: pl.BlockSpec(a.shape, lambda b, t: (0,) * a.ndim)
    out_shape = [jax.ShapeDtypeStruct((B, T, D_MODEL), BF16)]
    out_specs = [pl.BlockSpec((None, tm, D_MODEL), row)]
    if emit_v:
        out_shape.append(jax.ShapeDtypeStruct((B, T, D_MODEL), F32))
        out_specs.append(pl.BlockSpec((None, tm, D_MODEL), row))
    return pl.pallas_call(
        functools.partial(_sgu_kernel, emit_v=emit_v),
        grid=(B, T // tm),
        in_specs=[pl.BlockSpec((None, tm, D_MODEL), row), w(g), w(win), w(lng), w(lnb), w(ws), w(bias)],
        out_specs=out_specs,
        out_shape=out_shape,
        compiler_params=_cparams(2),
        name="sgu",
    )(h, g, win, lng, lnb, ws, bias)


def _pad_axis(a, axis, size):
    pad = [(0, 0)] * a.ndim
    pad[axis] = (0, size - a.shape[axis])
    return jnp.pad(a, pad)


def _sample_attention(qs, ksb, vsb, qis, kws, kisb, page_table, cache_k, cache_v, cache_ki, Bs, Ts):
    Ns = Bs * Ts
    past = page_table.shape[1] * PAGE_SIZE
    qi_pad = _pad_axis(qis.reshape(Bs, Ts, N_IDX_HEADS, IDX_DIM).transpose(0, 2, 1, 3), 2, SUBLANES)
    qi_pad = qi_pad.reshape(Bs, N_IDX_HEADS * SUBLANES, IDX_DIM)
    wcol = kws[0, :, IDX_DIM:IDX_DIM + N_IDX_HEADS].reshape(Bs, Ts, N_IDX_HEADS).transpose(0, 2, 1)
    wcol = _pad_axis(wcol, 2, SUBLANES).reshape(Bs, N_IDX_HEADS * SUBLANES, 1)
    wcol = jnp.broadcast_to(wcol, (Bs, N_IDX_HEADS * SUBLANES, LANES))
    kinew = _pad_axis(kisb.reshape(Bs, Ts, IDX_DIM), 1, PAGE_SIZE)
    n_sel = min(TOPK_MAX, (past + Ts) // 4)
    sc, thr, xsel = _sel_sample(page_table, qi_pad, wcol, cache_ki, kinew, n_sel)
    q_pad = qs.reshape(Bs, Ts, N_KV_HEADS, HEADS_PER_KV, HEAD_DIM).transpose(0, 2, 3, 1, 4)
    q_pad = _pad_axis(q_pad, 3, SUBLANES).reshape(Bs, N_KV_HEADS, HEADS_PER_KV * SUBLANES, HEAD_DIM)
    knew = _pad_axis(ksb.reshape(Bs, Ts, KV_W), 1, PAGE_SIZE)
    vnew = _pad_axis(vsb.reshape(Bs, Ts, KV_W), 1, PAGE_SIZE)
    o_s = _attn_sample(page_table, q_pad, sc, thr, xsel, cache_k, cache_v, knew, vnew)
    o_s = o_s.reshape(Bs, N_KV_HEADS, HEADS_PER_KV, SUBLANES, HEAD_DIM)[:, :, :, :Ts]
    return o_s.transpose(0, 3, 1, 2, 4).reshape(1, Ns, Q_W).astype(BF16)


def kernel(x_prompt, x_sample, cache_k, cache_v, cache_idx_k, state_ffn_conv, page_table, attn_norm,
           w_attn_in, w_attn_out, sgu_norm, w_sgu_in, sgu_ln_g, sgu_ln_b, sgu_ws, sgu_bs, w_sgu_out,
           ffn_norm, w_ffn_up, ffn_conv_w, ffn_conv_b, w_ffn_down, final_norm):
    B, S, _ = x_prompt.shape
    Bs, Ts, _ = x_sample.shape
    Ns = Bs * Ts
    n_pages = page_table.shape[1]
    past = n_pages * PAGE_SIZE
    n_pool = cache_k.shape[1]
    assert w_ffn_up.shape[0] == 2 and w_attn_in.shape[0] == 1 and w_sgu_in.shape[0] == 1
    assert Ns == CHUNK and Ts <= SUBLANES and S % 512 == 0 and n_pages % PAGES_PER_STEP == 0
    bf = lambda a: a.astype(BF16)
    r1 = lambda a: a[None]
    tm_p, tm_s = 512, Ns

    w_in = bf(_pad_axis(w_attn_in[0], 1, ATTN_IN_PAD))
    g_attn = r1(attn_norm[0])
    q, k32, v32, kb, vb, qi, kw, kib = _attn_in(x_prompt, jnp.arange(S), g_attn, w_in, tm_p)
    wi_t = jnp.swapaxes(kw[..., IDX_DIM:IDX_DIM + N_IDX_HEADS], 1, 2)
    o_p = _attn_prompt(q, kb, vb, qi, kib, wi_t)

    xs = x_sample.reshape(1, Ns, D_MODEL)
    pos_s = past + (jnp.arange(Ns) % Ts)
    qs, ks32, vs32, ksb, vsb, qis, kws, kisb = _attn_in(xs, pos_s, g_attn, w_in, tm_s)
    o_s = _sample_attention(qs, ksb, vsb, qis, kws, kisb, page_table,
                            cache_k[0].reshape(n_pool, PAGE_SIZE, KV_W),
                            cache_v[0].reshape(n_pool, PAGE_SIZE, KV_W), cache_idx_k[0], Bs, Ts)

    def ffn(i, h, o, wout, gfin, sample):
        state = None
        if sample:
            state = (jnp.repeat(state_ffn_conv[i][:, 1], Ts, axis=0),
                     jnp.repeat(state_ffn_conv[i][:, 0], Ts, axis=0))
        return _ffn(h, o, bf(wout), r1(ffn_norm[i]), bf(w_ffn_up[i]), ffn_conv_w[i], r1(ffn_conv_b[i]),
                    bf(w_ffn_down[i]), gfin, state, tm_s if sample else tm_p)

    hp, cst_p0 = ffn(0, x_prompt, o_p, w_attn_out[0], None, False)
    hs, up_s0 = ffn(0, xs, o_s, w_attn_out[0], None, True)

    sgu_args = (r1(sgu_norm[0]), bf(w_sgu_in[0]), r1(sgu_ln_g[0]), r1(sgu_ln_b[0]))
    (y_p,) = _sgu(hp, *sgu_args, sgu_ws[0], _sgu_bias(sgu_bs[0]), tm_p, False)
    ws_s, bias_s = _sgu_sample_weights(sgu_ws[0], sgu_bs[0], Bs, Ts)
    y_s, vn_s = _sgu(hs, *sgu_args, ws_s, bias_s, tm_s, True)

    gfin = r1(final_norm)
    yp, cst_p1 = ffn(1, hp, y_p, w_sgu_out[0], gfin, False)
    ys, up_s1 = ffn(1, hs, y_s, w_sgu_out[0], gfin, True)

    tail = lambda c: c[:, SUBLANES - (CONV_W - 1):]
    stail = lambda u: u.reshape(Bs, Ts, 2 * D_FF)[:, Ts - (CONV_W - 1):]
    return (yp, ys.reshape(Bs, Ts, D_MODEL),
            k32.reshape(1, B, S, N_KV_HEADS, HEAD_DIM), v32.reshape(1, B, S, N_KV_HEADS, HEAD_DIM),
            kw[None, ..., :IDX_DIM],
            ks32.reshape(1, Bs, Ts, N_KV_HEADS, HEAD_DIM), vs32.reshape(1, Bs, Ts, N_KV_HEADS, HEAD_DIM),
            kws[..., :IDX_DIM].reshape(1, Bs, Ts, IDX_DIM),
            vn_s.reshape(1, Bs, Ts, D_MODEL),
            jnp.stack([tail(cst_p0), tail(cst_p1)]),
            jnp.stack([stail(up_s0), stail(up_s1)]))
```

```python
import functools

import numpy as np
import jax
import jax.numpy as jnp
from jax import lax
from jax.experimental import pallas as pl
from jax.experimental.pallas import tpu as pltpu

F32 = jnp.float32
BF16 = jnp.bfloat16

D_MODEL = 1024
N_HEADS = 8
HEAD_DIM = 128
N_KV_HEADS = 2
HEADS_PER_KV = N_HEADS // N_KV_HEADS
ROT_DIM = 32
ROPE_THETA = 500000.0
N_IDX_HEADS = 4
IDX_DIM = 64
IDX_ROT_DIM = 16
TOPK_MAX = 256
PAGE_SIZE = 128
Q_W = N_HEADS * HEAD_DIM
KV_W = N_KV_HEADS * HEAD_DIM
QI_W = N_IDX_HEADS * IDX_DIM
ATTN_IN = Q_W + 2 * KV_W + QI_W + IDX_DIM + N_IDX_HEADS
ATTN_IN_PAD = 1920
KW_COL = Q_W + 2 * KV_W + QI_W
CHUNK = 128
SGU_GROUPS = 4
SGU_GROUP_DIM = D_MODEL // SGU_GROUPS
D_FF = 2816
CONV_W = 3
NORM_EPS = 1e-6
LN_EPS = 1e-5
LOG2E = 1.4426950408889634
ATTN_SCALE_LOG2 = HEAD_DIM ** -0.5 * LOG2E
W_IDX_SCALE = (N_IDX_HEADS ** -0.5) * (IDX_DIM ** -0.5)

LANES = 128
SUBLANES = 8
KB = 128
SUPER = 512
ATTN_ROW_PARTS = 2
FF_CHUNK = 256
N_FF_CHUNKS = D_FF // FF_CHUNK
PAGES_PER_STEP = 8
SEL_ROWS = 64
N_BISECT = 20
VMEM_LIMIT = 56 * 1024 * 1024
MASK_BIAS = -1e30

_NT = (((1,), (1,)), ((), ()))


def _cparams(n_axes):
    return pltpu.CompilerParams(
        dimension_semantics=("arbitrary",) * n_axes, vmem_limit_bytes=VMEM_LIMIT)


def _rmsnorm(x, g):
    return x * lax.rsqrt(jnp.mean(x * x, axis=-1, keepdims=True) + NORM_EPS) * g


def _rope(u, c, sa, sb, half):
    return u * c + pltpu.roll(u, LANES - half, 1) * sa + pltpu.roll(u, half, 1) * sb


def _attn_in_kernel(x_ref, g_ref, w_ref, ch_ref, sha_ref, shb_ref, ci_ref, sia_ref, sib_ref,
                    ckw_ref, skwa_ref, skwb_ref,
                    q_ref, k_ref, v_ref, kb_ref, vb_ref, qi_ref, kw_ref, kib_ref):
    xn = _rmsnorm(x_ref[...], g_ref[...]).astype(BF16)
    z = jnp.dot(xn, w_ref[...], preferred_element_type=F32)
    ch, sha, shb = ch_ref[...], sha_ref[...], shb_ref[...]
    for h in range(N_HEADS):
        sl = slice(h * HEAD_DIM, (h + 1) * HEAD_DIM)
        q_ref[:, sl] = _rope(z[:, sl], ch, sha, shb, ROT_DIM // 2).astype(BF16)
    for h in range(N_KV_HEADS):
        sl = slice(h * HEAD_DIM, (h + 1) * HEAD_DIM)
        kk = _rope(z[:, Q_W + h * HEAD_DIM:Q_W + (h + 1) * HEAD_DIM], ch, sha, shb, ROT_DIM // 2)
        k_ref[:, sl] = kk
        kb_ref[:, sl] = kk.astype(BF16)
    v = z[:, Q_W + KV_W:Q_W + 2 * KV_W]
    v_ref[...] = v
    vb_ref[...] = v.astype(BF16)
    ci, sia, sib = ci_ref[...], sia_ref[...], sib_ref[...]
    qi0 = Q_W + 2 * KV_W
    for s in range(QI_W // LANES):
        sl = slice(s * LANES, (s + 1) * LANES)
        qi_ref[:, sl] = _rope(z[:, qi0 + s * LANES:qi0 + (s + 1) * LANES], ci, sia, sib,
                              IDX_ROT_DIM // 2).astype(BF16)
    kw = _rope(z[:, KW_COL:KW_COL + LANES], ckw_ref[...], skwa_ref[...], skwb_ref[...],
               IDX_ROT_DIM // 2)
    kw_ref[...] = kw
    kib_ref[...] = kw[:, :IDX_DIM].astype(BF16)


def _rope_tables(pos, rot, period, tail_scale=None):
    half = rot // 2
    freqs = ROPE_THETA ** (-(jnp.arange(half, dtype=F32) * 2.0) / rot)
    ang = pos.astype(F32)[:, None] * freqs[None, :]
    cos, sin = jnp.cos(ang), jnp.sin(ang)
    lane = np.arange(LANES) % period
    col = lane % half
    c = jnp.where(lane < rot, cos[:, col], 1.0)
    sa = jnp.where(lane < half, -sin[:, col], 0.0)
    sb = jnp.where((lane >= half) & (lane < rot), sin[:, col], 0.0)
    if tail_scale is not None:
        glane = np.arange(LANES)
        c = jnp.where(glane < period, c, 1.0)
        c = jnp.where((glane >= IDX_DIM) & (glane < IDX_DIM + N_IDX_HEADS), tail_scale, c)
        sa = jnp.where(glane < period, sa, 0.0)
        sb = jnp.where(glane < period, sb, 0.0)
    return c.astype(F32), sa.astype(F32), sb.astype(F32)


def _attn_in(x, pos, g, w_pad, tm):
    B, T, _ = x.shape
    ch, sha, shb = _rope_tables(pos, ROT_DIM, HEAD_DIM)
    ci, sia, sib = _rope_tables(pos, IDX_ROT_DIM, IDX_DIM)
    ckw, skwa, skwb = _rope_tables(pos, IDX_ROT_DIM, IDX_DIM, tail_scale=W_IDX_SCALE)
    row = lambda b, t: (b, t, 0)
    tab = lambda b, t: (t, 0)
    const = lambda b, t: (0, 0)
    blk = lambda w: pl.BlockSpec((None, tm, w), row)
    tspec = pl.BlockSpec((tm, LANES), tab)
    outs = pl.pallas_call(
        _attn_in_kernel,
        grid=(B, T // tm),
        in_specs=[blk(D_MODEL), pl.BlockSpec((1, D_MODEL), const),
                  pl.BlockSpec((D_MODEL, ATTN_IN_PAD), const),
                  tspec, tspec, tspec, tspec, tspec, tspec, tspec, tspec, tspec],
        out_specs=[blk(Q_W), blk(KV_W), blk(KV_W), blk(KV_W), blk(KV_W), blk(QI_W),
                   blk(LANES), blk(IDX_DIM)],
        out_shape=[jax.ShapeDtypeStruct((B, T, Q_W), BF16),
                   jax.ShapeDtypeStruct((B, T, KV_W), F32),
                   jax.ShapeDtypeStruct((B, T, KV_W), F32),
                   jax.ShapeDtypeStruct((B, T, KV_W), BF16),
                   jax.ShapeDtypeStruct((B, T, KV_W), BF16),
                   jax.ShapeDtypeStruct((B, T, QI_W), BF16),
                   jax.ShapeDtypeStruct((B, T, LANES), F32),
                   jax.ShapeDtypeStruct((B, T, IDX_DIM), BF16)],
        compiler_params=_cparams(2),
        name="attn_in",
    )(x, g, w_pad, ch, sha, shb, ci, sia, sib, ckw, skwa, skwb)
    return outs


def _search_threshold(count_ge, bracket, scan, lo0, hi0, n0, kf):
    hi0 = jnp.where(n0 <= kf, lo0, hi0)

    def coarse(_, st):
        lo, hi = st
        p = 0.5 * lo + 0.5 * hi
        ge = count_ge(p) >= kf
        return jnp.where(ge, p, lo), jnp.where(ge, hi, p)

    lo, hi = lax.fori_loop(0, N_BISECT, coarse, (lo0, hi0))
    a, b = bracket(lo, hi)

    def n_open(a, b):
        return jnp.max(jnp.where(a < b, 1, 0).astype(jnp.int32))

    def body(carry):
        _, a, b = carry
        p = jnp.minimum(jnp.maximum(0.5 * a + 0.5 * b, a), b)
        p = jnp.where(p > a, p, b)
        cnt, lowmax, highmin = scan(p)
        act = a < b
        a2 = jnp.where(act, jnp.where(cnt >= kf, highmin, a), a)
        b2 = jnp.where(act, jnp.where(cnt >= kf, b, lowmax), b)
        return n_open(a2, b2), a2, b2

    out = lax.while_loop(lambda c: c[0] > 0, body, (n_open(a, b), a, b))
    return out[1]


ACC_ROWS = 64


def _rows_reduce8(x, op):
    return op(x.reshape(x.shape[0] // ACC_ROWS, ACC_ROWS, LANES), axis=0)


def _attn_prompt_kernel(q_ref, kb_ref, vb_ref, qi_ref, kib_ref, wi_ref, o_ref,
                        sc_ref, bias_ref, vext_ref, tri_ref, m_ref, acc_ref, *, n_sel):
    i = pl.program_id(1)
    nsup = i // (SUPER // KB) + 1
    kf = float(n_sel)
    inf = jnp.float32(jnp.inf)

    @pl.when(i == 0)
    def _():
        for g in range(N_KV_HEADS):
            vext_ref[g, :, :HEAD_DIM] = vb_ref[:, g * HEAD_DIM:(g + 1) * HEAD_DIM]
            vext_ref[g, :, HEAD_DIM:] = jnp.ones((vext_ref.shape[1], HEAD_DIM), BF16)
        r = lax.broadcasted_iota(jnp.int32, (SUPER, SUPER), 0)
        c = lax.broadcasted_iota(jnp.int32, (SUPER, SUPER), 1)
        tri_ref[...] = jnp.where(c <= r, 1.0, 0.0).astype(BF16)

    def rows(u):
        return pl.ds(pl.multiple_of(u * SUPER, SUPER), SUPER)

    def sweep(fn, init):
        return lax.fori_loop(0, nsup, lambda u, c: fn(sc_ref[rows(u), :], c), init)

    z8 = jnp.zeros((ACC_ROWS, LANES), F32)
    fold = lambda acc8, op: op(acc8, axis=0, keepdims=True)

    w = wi_ref[...]
    qi = qi_ref[...]
    qih = [qi[:, h * IDX_DIM:(h + 1) * IDX_DIM] for h in range(N_IDX_HEADS)]
    kmq = (lax.broadcasted_iota(jnp.int32, (SUPER, KB), 0)
           - lax.broadcasted_iota(jnp.int32, (SUPER, KB), 1))

    def p1(u, carry):
        amin, amax = carry
        kic = kib_ref[rows(u), :]
        s = None
        for h in range(N_IDX_HEADS):
            sh = lax.dot_general(kic, qih[h], _NT, preferred_element_type=F32)
            t = w[h:h + 1, :] * jnp.maximum(sh, 0.0)
            s = t if s is None else s + t
        causal = kmq <= i * KB - u * SUPER
        x = jnp.where(causal, s, -inf)
        sc_ref[rows(u), :] = x
        amin = jnp.minimum(amin, _rows_reduce8(jnp.where(causal, s, inf), jnp.min))
        amax = jnp.maximum(amax, _rows_reduce8(x, jnp.max))
        return amin, amax

    amin8, amax8 = lax.fori_loop(0, nsup, p1, (z8 + inf, z8 - inf))
    lane = lax.broadcasted_iota(jnp.int32, (1, LANES), 1)
    n0 = (i * KB + lane + 1).astype(F32)

    def count_ge(p):
        return fold(sweep(lambda x, c: c + _rows_reduce8(jnp.where(x >= p, 1.0, 0.0), jnp.sum), z8),
                    jnp.sum)

    def bracket(lo, hi):
        def f(x, c):
            return (jnp.minimum(c[0], _rows_reduce8(jnp.where(x >= lo, x, inf), jnp.min)),
                    jnp.maximum(c[1], _rows_reduce8(jnp.where(x <= hi, x, -inf), jnp.max)))
        a8, b8 = sweep(f, (z8 + inf, z8 - inf))
        return fold(a8, jnp.min), fold(b8, jnp.max)

    def scan(p):
        def f(x, c):
            ge = x >= p
            return (c[0] + _rows_reduce8(jnp.where(ge, 1.0, 0.0), jnp.sum),
                    jnp.maximum(c[1], _rows_reduce8(jnp.where(ge, -inf, x), jnp.max)),
                    jnp.minimum(c[2], _rows_reduce8(jnp.where(ge, x, inf), jnp.min)))
        cnt, lo, hi = sweep(f, (z8, z8 - inf, z8 + inf))
        return fold(cnt, jnp.sum), fold(lo, jnp.max), fold(hi, jnp.min)

    thr = _search_threshold(count_ge, bracket, scan, fold(amin8, jnp.min), fold(amax8, jnp.max),
                            n0, kf)

    cgt = fold(sweep(lambda x, c: c + _rows_reduce8(jnp.where(x > thr, 1.0, 0.0), jnp.sum), z8),
               jnp.sum)
    need = kf - cgt

    def p3(u, run):
        x = sc_ref[rows(u), :]
        eq = x == thr
        rank = jnp.dot(tri_ref[...], jnp.where(eq, 1.0, 0.0).astype(BF16),
                       preferred_element_type=F32) + run
        keep_tie = jnp.where(rank <= need, 0.0, MASK_BIAS)
        bias = jnp.where(x > thr, 0.0, jnp.where(eq, keep_tie, MASK_BIAS))
        bias_ref[rows(u), :] = bias.astype(BF16)
        return rank[SUPER - 1:SUPER, :]

    lax.fori_loop(0, nsup, p3, jnp.zeros((1, LANES), F32))

    eye = jnp.where(lax.broadcasted_iota(jnp.int32, (KB, KB), 0)
                    == lax.broadcasted_iota(jnp.int32, (KB, KB), 1), 1.0, 0.0).astype(BF16)
    eye4 = jnp.concatenate([eye] * HEADS_PER_KV, axis=0)
    ags = []
    for g in range(N_KV_HEADS):
        qg = jnp.concatenate(
            [q_ref[:, (g * HEADS_PER_KV + r) * HEAD_DIM:(g * HEADS_PER_KV + r + 1) * HEAD_DIM]
             for r in range(HEADS_PER_KV)], axis=0)
        ags.append(jnp.concatenate([qg, eye4], axis=1))
    m_ref[...] = jnp.full(m_ref.shape, -inf, F32)
    acc_ref[...] = jnp.zeros(acc_ref.shape, F32)
    nrow = HEADS_PER_KV * KB
    part = nrow // ATTN_ROW_PARTS

    def pa(u, c):
        bias_t = bias_ref[rows(u), :]
        for g in range(N_KV_HEADS):
            bmat = jnp.concatenate([kb_ref[rows(u), g * HEAD_DIM:(g + 1) * HEAD_DIM], bias_t],
                                   axis=1)
            vx = vext_ref[g, rows(u), :]
            for hp in range(ATTN_ROW_PARTS):
                rs = slice(hp * part, (hp + 1) * part)
                t = lax.dot_general(ags[g][rs], bmat, _NT,
                                    preferred_element_type=F32) * ATTN_SCALE_LOG2
                m_old = m_ref[g, rs]
                m_new = jnp.maximum(m_old, jnp.max(t, axis=1, keepdims=True))
                p = jnp.exp2(t - m_new).astype(BF16)
                acc_ref[g, rs] = (jnp.exp2(m_old - m_new) * acc_ref[g, rs]
                                  + jnp.dot(p, vx, preferred_element_type=F32))
                m_ref[g, rs] = m_new
        return c

    lax.fori_loop(0, nsup, pa, 0)
    for g in range(N_KV_HEADS):
        acc = acc_ref[g]
        o = acc[:, :HEAD_DIM] / acc[:, HEAD_DIM:]
        for r in range(HEADS_PER_KV):
            h = g * HEADS_PER_KV + r
            o_ref[:, h * HEAD_DIM:(h + 1) * HEAD_DIM] = o[r * KB:(r + 1) * KB].astype(BF16)


def _attn_prompt(q, kb, vb, qi, kib, wi_t):
    B, S, _ = q.shape
    n_sel = min(TOPK_MAX, S // 4)
    qblk = lambda w: pl.BlockSpec((None, KB, w), lambda b, i: (b, i, 0))
    full = lambda w: pl.BlockSpec((None, S, w), lambda b, i: (b, 0, 0))
    nrow = HEADS_PER_KV * KB
    return pl.pallas_call(
        functools.partial(_attn_prompt_kernel, n_sel=n_sel),
        grid=(B, S // KB),
        in_specs=[qblk(Q_W), full(KV_W), full(KV_W), qblk(QI_W), full(IDX_DIM),
                  pl.BlockSpec((None, N_IDX_HEADS, KB), lambda b, i: (b, 0, i))],
        out_specs=qblk(Q_W),
        out_shape=jax.ShapeDtypeStruct((B, S, Q_W), BF16),
        scratch_shapes=[pltpu.VMEM((S, KB), F32), pltpu.VMEM((S, KB), BF16),
                        pltpu.VMEM((N_KV_HEADS, S, 2 * HEAD_DIM), BF16),
                        pltpu.VMEM((SUPER, SUPER), BF16),
                        pltpu.VMEM((N_KV_HEADS, nrow, 1), F32),
                        pltpu.VMEM((N_KV_HEADS, nrow, 2 * HEAD_DIM), F32)],
        compiler_params=_cparams(2),
        name="attn_prompt",
    )(q, kb, vb, qi, kib, wi_t)


def _idx_scores(qi, wcol, keys_t):
    s = jnp.dot(qi, keys_t, preferred_element_type=F32)
    r = jnp.maximum(s, 0.0) * wcol
    out = r[0:SUBLANES]
    for h in range(1, N_IDX_HEADS):
        out = out + r[h * SUBLANES:(h + 1) * SUBLANES]
    return out


def _score_sample_kernel(pt_ref, qi_ref, w_ref, *refs, past):
    ki_refs = refs[:PAGES_PER_STEP]
    kinew_ref, sc_ref = refs[PAGES_PER_STEP:]
    j = pl.program_id(1)
    qi = qi_ref[...]
    wcol = w_ref[...]
    for p in range(PAGES_PER_STEP):
        col = pl.multiple_of((j * PAGES_PER_STEP + p) * PAGE_SIZE, PAGE_SIZE)
        sc_ref[:, pl.ds(col, PAGE_SIZE)] = _idx_scores(qi, wcol, ki_refs[p][...].astype(BF16))

    @pl.when(j == pl.num_programs(1) - 1)
    def _():
        inf = jnp.float32(jnp.inf)
        row = lax.broadcasted_iota(jnp.int32, (SUBLANES, PAGE_SIZE), 0)
        lane = lax.broadcasted_iota(jnp.int32, (SUBLANES, PAGE_SIZE), 1)
        snew = _idx_scores(qi, wcol, kinew_ref[...])
        sc_ref[:, past:past + PAGE_SIZE] = jnp.where(lane <= row, snew, -inf)
        sc_ref[:, past + PAGE_SIZE:] = jnp.full((SUBLANES, SUPER - PAGE_SIZE), -inf, F32)


def _score_sample(page_table, qi_pad, wcol, cache_ki_t, kinew_t):
    B, n_pages = page_table.shape
    past = n_pages * PAGE_SIZE
    steps = n_pages // PAGES_PER_STEP
    page = lambda p: pl.BlockSpec((None, IDX_DIM, PAGE_SIZE),
                                  lambda b, j, pt: (pt[b, j * PAGES_PER_STEP + p], 0, 0))
    per_b = lambda r, w: pl.BlockSpec((None, r, w), lambda b, j, pt: (b, 0, 0))
    return pl.pallas_call(
        functools.partial(_score_sample_kernel, past=past),
        grid_spec=pltpu.PrefetchScalarGridSpec(
            num_scalar_prefetch=1,
            grid=(B, steps),
            in_specs=[per_b(N_IDX_HEADS * SUBLANES, IDX_DIM), per_b(N_IDX_HEADS * SUBLANES, LANES)]
                     + [page(p) for p in range(PAGES_PER_STEP)]
                     + [per_b(IDX_DIM, PAGE_SIZE)],
            out_specs=per_b(SUBLANES, past + SUPER),
        ),
        out_shape=jax.ShapeDtypeStruct((B, SUBLANES, past + SUPER), F32),
        compiler_params=_cparams(2),
        name="score_sample",
    )(page_table, qi_pad, wcol, *([cache_ki_t] * PAGES_PER_STEP), kinew_t)


def _lanes_reduce(x, op):
    out = x[:, :LANES]
    for t in range(1, x.shape[1] // LANES):
        out = op(out, x[:, t * LANES:(t + 1) * LANES])
    return out


def _select_sample_kernel(sc_ref, sel_ref, tri_ref, *, n_sel):
    R, L = sc_ref.shape
    nsup = L // SUPER
    kf = float(n_sel)
    inf = jnp.float32(jnp.inf)
    r = lax.broadcasted_iota(jnp.int32, (SUPER, SUPER), 0)
    c = lax.broadcasted_iota(jnp.int32, (SUPER, SUPER), 1)
    tri_ref[...] = jnp.where(r <= c, 1.0, 0.0).astype(BF16)

    def cols(u):
        return pl.ds(pl.multiple_of(u * SUPER, SUPER), SUPER)

    def sweep(fn, init):
        return lax.fori_loop(0, nsup, lambda u, cr: fn(sc_ref[:, cols(u)], cr), init)

    zr = jnp.zeros((R, LANES), F32)
    rsum = lambda a: jnp.sum(a, axis=1, keepdims=True)
    rmax = lambda a: jnp.max(a, axis=1, keepdims=True)
    rmin = lambda a: jnp.min(a, axis=1, keepdims=True)
    one = lambda m: jnp.where(m, 1.0, 0.0)

    def f0(x, cr):
        fin = x > -inf
        return (jnp.minimum(cr[0], _lanes_reduce(jnp.where(fin, x, inf), jnp.minimum)),
                jnp.maximum(cr[1], _lanes_reduce(x, jnp.maximum)),
                cr[2] + _lanes_reduce(one(fin), jnp.add))
    lo8, hi8, n8 = sweep(f0, (zr + inf, zr - inf, zr))

    def count_ge(p):
        return rsum(sweep(lambda x, cr: cr + _lanes_reduce(one(x >= p), jnp.add), zr))

    def bracket(lo, hi):
        def f(x, cr):
            return (jnp.minimum(cr[0], _lanes_reduce(jnp.where(x >= lo, x, inf), jnp.minimum)),
                    jnp.maximum(cr[1], _lanes_reduce(jnp.where(x <= hi, x, -inf), jnp.maximum)))
        a8, b8 = sweep(f, (zr + inf, zr - inf))
        return rmin(a8), rmax(b8)

    def scan(p):
        def f(x, cr):
            ge = x >= p
            return (cr[0] + _lanes_reduce(one(ge), jnp.add),
                    jnp.maximum(cr[1], _lanes_reduce(jnp.where(ge, -inf, x), jnp.maximum)),
                    jnp.minimum(cr[2], _lanes_reduce(jnp.where(ge, x, inf), jnp.minimum)))
        cnt, lo, hi = sweep(f, (zr, zr - inf, zr + inf))
        return rsum(cnt), rmax(lo), rmin(hi)

    thr = _search_threshold(count_ge, bracket, scan, rmin(lo8), rmax(hi8), rsum(n8), kf)
    need = kf - rsum(sweep(lambda x, cr: cr + _lanes_reduce(one(x > thr), jnp.add), zr))

    def p3(u, run):
        x = sc_ref[:, cols(u)]
        eq = x == thr
        rank = jnp.dot(one(eq).astype(BF16), tri_ref[...], preferred_element_type=F32) + run
        sel_ref[:, cols(u)] = jnp.where(x > thr, 1.0, jnp.where(eq, one(rank <= need), 0.0))
        return rank[:, SUPER - 1:SUPER]

    lax.fori_loop(0, nsup, p3, jnp.zeros((R, 1), F32))


def _select_sample(sc, n_sel):
    R, L = sc.shape
    spec = pl.BlockSpec((SEL_ROWS, L), lambda i: (i, 0))
    return pl.pallas_call(
        functools.partial(_select_sample_kernel, n_sel=n_sel),
        grid=(R // SEL_ROWS,),
        in_specs=[spec],
        out_specs=spec,
        out_shape=jax.ShapeDtypeStruct((R, L), F32),
        scratch_shapes=[pltpu.VMEM((SUPER, SUPER), BF16)],
        compiler_params=_cparams(1),
        name="select_sample",
    )(sc)


def _attn_sample_kernel(pt_ref, q_ref, sel_ref, *refs, past):
    k_refs = refs[:PAGES_PER_STEP]
    v_refs = refs[PAGES_PER_STEP:2 * PAGES_PER_STEP]
    knew_ref, vnew_ref, o_ref, m_ref, acc_ref = refs[2 * PAGES_PER_STEP:]
    j = pl.program_id(1)
    nrow = N_HEADS * SUBLANES
    pw = N_KV_HEADS * PAGE_SIZE

    @pl.when(j == 0)
    def _():
        m_ref[...] = jnp.full(m_ref.shape, -jnp.inf, F32)
        acc_ref[...] = jnp.zeros(acc_ref.shape, F32)

    expand = jnp.where(lax.broadcasted_iota(jnp.int32, (PAGE_SIZE, pw), 1) // N_KV_HEADS
                       == lax.broadcasted_iota(jnp.int32, (PAGE_SIZE, pw), 0), 1.0, 0.0).astype(BF16)
    own = (lax.broadcasted_iota(jnp.int32, (nrow, pw), 0) // (HEADS_PER_KV * SUBLANES)
           == lax.broadcasted_iota(jnp.int32, (nrow, pw), 1) % N_KV_HEADS)
    q = q_ref[...]
    ones = jnp.ones((pw, HEAD_DIM), BF16)

    def scores(kk, col):
        s = lax.dot_general(q, kk, _NT, preferred_element_type=F32) * ATTN_SCALE_LOG2
        sel = jnp.dot(sel_ref[:, pl.ds(col, PAGE_SIZE)].astype(BF16), expand,
                      preferred_element_type=F32)
        sel = jnp.concatenate([sel] * N_HEADS, axis=0)
        return jnp.where(own, jnp.where(sel > 0.5, s, MASK_BIAS), MASK_BIAS)

    def update(s_list, v_list):
        s = jnp.concatenate(s_list, axis=1)
        m_old = m_ref[...]
        m_new = jnp.maximum(m_old, jnp.max(s, axis=1, keepdims=True))
        pb = jnp.exp2(s - m_new).astype(BF16)
        acc = jnp.exp2(m_old - m_new) * acc_ref[...]
        for n, vv in enumerate(v_list):
            acc = acc + jnp.dot(pb[:, n * pw:(n + 1) * pw], jnp.concatenate([vv, ones], axis=1),
                                preferred_element_type=F32)
        acc_ref[...] = acc
        m_ref[...] = m_new

    s_list, v_list = [], []
    for p in range(PAGES_PER_STEP):
        col = pl.multiple_of((j * PAGES_PER_STEP + p) * PAGE_SIZE, PAGE_SIZE)
        s_list.append(scores(k_refs[p][...].astype(BF16), col))
        v_list.append(v_refs[p][...].astype(BF16))
    update(s_list, v_list)

    @pl.when(j == pl.num_programs(1) - 1)
    def _():
        update([scores(knew_ref[...], past)], [vnew_ref[...]])
        acc = acc_ref[...]
        o_ref[...] = acc[:, :HEAD_DIM] / acc[:, HEAD_DIM:]


def _attn_sample(page_table, q_all, sel, cache_k, cache_v, knew, vnew):
    B, n_pages = page_table.shape
    past = n_pages * PAGE_SIZE
    steps = n_pages // PAGES_PER_STEP
    nrow = N_HEADS * SUBLANES
    pw = N_KV_HEADS * PAGE_SIZE
    page = lambda p: pl.BlockSpec((pw, HEAD_DIM), lambda b, j, pt: (pt[b, j * PAGES_PER_STEP + p], 0))
    per_b = lambda r, w: pl.BlockSpec((None, r, w), lambda b, j, pt: (b, 0, 0))
    return pl.pallas_call(
        functools.partial(_attn_sample_kernel, past=past),
        grid_spec=pltpu.PrefetchScalarGridSpec(
            num_scalar_prefetch=1,
            grid=(B, steps),
            in_specs=[per_b(nrow, HEAD_DIM), per_b(SUBLANES, sel.shape[2])]
                     + [page(p) for p in range(PAGES_PER_STEP)] * 2
                     + [per_b(pw, HEAD_DIM), per_b(pw, HEAD_DIM)],
            out_specs=per_b(nrow, HEAD_DIM),
            scratch_shapes=[pltpu.VMEM((nrow, 1), F32), pltpu.VMEM((nrow, 2 * HEAD_DIM), F32)],
        ),
        out_shape=jax.ShapeDtypeStruct((B, nrow, HEAD_DIM), F32),
        compiler_params=_cparams(2),
        name="attn_sample",
    )(page_table, q_all, sel, *([cache_k] * PAGES_PER_STEP), *([cache_v] * PAGES_PER_STEP), knew, vnew)


def _ffn_cols(a):
    lead = a.shape[:-1]
    a = a.reshape(lead + (2, N_FF_CHUNKS, FF_CHUNK))
    return jnp.swapaxes(a, -3, -2).reshape(lead + (2 * D_FF,))


def _ffn_cols_inv(a):
    lead = a.shape[:-1]
    a = a.reshape(lead + (N_FF_CHUNKS, 2, FF_CHUNK))
    return jnp.swapaxes(a, -3, -2).reshape(lead + (2 * D_FF,))


def _ffn_kernel(*refs, sample, final):
    it = iter(refs)
    h_ref, o_ref, wout_ref, g_ref, wup_ref, cw_ref, cb_ref, wdown_ref = (next(it) for _ in range(8))
    gfin_ref = next(it) if final else None
    st1_ref, st2_ref = (next(it), next(it)) if sample else (None, None)
    hout_ref, cst_ref = next(it), next(it)
    act_ref = next(it)
    carry_ref = None if sample else next(it)

    tm = h_ref.shape[0]
    cwid = 2 * FF_CHUNK
    h1 = h_ref[...] + jnp.dot(o_ref[...], wout_ref[...], preferred_element_type=F32)
    xn = _rmsnorm(h1, g_ref[...]).astype(BF16)

    if sample:
        t4 = lax.broadcasted_iota(jnp.int32, (tm, cwid), 0) % 4
    else:
        @pl.when(pl.program_id(1) == 0)
        def _():
            carry_ref[...] = jnp.zeros(carry_ref.shape, F32)
        row8 = lax.broadcasted_iota(jnp.int32, (SUBLANES, cwid), 0)

    for c in range(N_FF_CHUNKS):
        cols = slice(c * cwid, (c + 1) * cwid)
        u = jnp.dot(xn, wup_ref[:, cols], preferred_element_type=F32)
        r1 = pltpu.roll(u, 1, 0)
        r2 = pltpu.roll(u, 2, 0)
        if sample:
            s1 = jnp.where(t4 == 0, st1_ref[:, cols], r1)
            s2 = jnp.where(t4 == 0, st2_ref[:, cols], jnp.where(t4 == 1, st1_ref[:, cols], r2))
            cst_ref[:, cols] = u
        else:
            prev = carry_ref[:, cols]
            top1 = jnp.where(row8 < 1, pltpu.roll(prev, 1, 0), r1[:SUBLANES])
            top2 = jnp.where(row8 < 2, pltpu.roll(prev, 2, 0), r2[:SUBLANES])
            s1 = jnp.concatenate([top1, r1[SUBLANES:]], axis=0)
            s2 = jnp.concatenate([top2, r2[SUBLANES:]], axis=0)
            carry_ref[:, cols] = u[tm - SUBLANES:]
            cst_ref[:, cols] = u[tm - SUBLANES:]
        cw = cw_ref[:, cols]
        cv = cb_ref[:, cols] + cw[0:1] * s2 + cw[1:2] * s1 + cw[2:3] * u
        gate = cv[:, :FF_CHUNK]
        act_ref[:, c * FF_CHUNK:(c + 1) * FF_CHUNK] = (
            gate * (1.0 / (1.0 + jnp.exp(-gate))) * cv[:, FF_CHUNK:]).astype(BF16)

    out = h1 + jnp.dot(act_ref[...], wdown_ref[...], preferred_element_type=F32)
    if final:
        out = _rmsnorm(out, gfin_ref[...])
    hout_ref[...] = out


def _ffn(h, o, wout, g, wup, cw, cb, wdown, gfin, state, tm):
    B, T, _ = h.shape
    sample = state is not None
    final = gfin is not None
    row = lambda b, t: (b, t, 0)
    const = lambda b, t: (0, 0)
    w = lambda a: pl.BlockSpec(a.shape, const, pipeline_mode=pl.Buffered(1))
    operands = [h, o, wout, g, wup, cw, cb, wdown]
    in_specs = [pl.BlockSpec((None, tm, D_MODEL), row), pl.BlockSpec((None, tm, D_MODEL), row),
                w(wout), w(g), w(wup), w(cw), w(cb), w(wdown)]
    if final:
        operands.append(gfin)
        in_specs.append(w(gfin))
    scratch = [pltpu.VMEM((tm, D_FF), BF16)]
    if sample:
        operands += [state[0], state[1]]
        in_specs += [w(state[0]), w(state[1])]
        cst_shape = jax.ShapeDtypeStruct((B, T, 2 * D_FF), F32)
        cst_spec = pl.BlockSpec((None, tm, 2 * D_FF), row)
    else:
        cst_shape = jax.ShapeDtypeStruct((B, SUBLANES, 2 * D_FF), F32)
        cst_spec = pl.BlockSpec((None, SUBLANES, 2 * D_FF), lambda b, t: (b, 0, 0))
        scratch.append(pltpu.VMEM((SUBLANES, 2 * D_FF), F32))
    return pl.pallas_call(
        functools.partial(_ffn_kernel, sample=sample, final=final),
        grid=(B, T // tm),
        in_specs=in_specs,
        out_specs=[pl.BlockSpec((None, tm, D_MODEL), row), cst_spec],
        out_shape=[jax.ShapeDtypeStruct((B, T, D_MODEL), F32), cst_shape],
        scratch_shapes=scratch,
        compiler_params=_cparams(2),
        name="ffn_sample" if sample else "ffn_prompt",
    )(*operands)


def _sgu_kernel(h_ref, g_ref, win_ref, lng_ref, lnb_ref, ws_ref, bias_ref, *out_refs, emit_v):
    y_ref = out_refs[0]
    tm = h_ref.shape[0]
    xn = _rmsnorm(h_ref[...], g_ref[...]).astype(BF16)
    uv = jnp.dot(xn, win_ref[...], preferred_element_type=F32)
    u = uv[:, :D_MODEL]
    v = uv[:, D_MODEL:]
    xc = v - jnp.mean(v, axis=-1, keepdims=True)
    vn = xc * lax.rsqrt(jnp.mean(xc * xc, axis=-1, keepdims=True) + LN_EPS) * lng_ref[...] + lnb_ref[...]
    if emit_v:
        out_refs[1][...] = vn
    vnb = vn.astype(BF16)
    tri = (lax.broadcasted_iota(jnp.int32, (CHUNK, CHUNK), 0)
           >= lax.broadcasted_iota(jnp.int32, (CHUNK, CHUNK), 1))
    for g in range(SGU_GROUPS):
        wg = jnp.where(tri, ws_ref[g], 0.0).astype(BF16)
        cols = slice(g * SGU_GROUP_DIM, (g + 1) * SGU_GROUP_DIM)
        for c in range(tm // CHUNK):
            rows = slice(c * CHUNK, (c + 1) * CHUNK)
            s = jnp.dot(wg, vnb[rows, cols], preferred_element_type=F32) + bias_ref[:, cols]
            y_ref[rows, cols] = (u[rows, cols] * s).astype(BF16)


def _sgu_bias(bs):
    return jnp.repeat(bs.T, SGU_GROUP_DIM, axis=1)


def _sgu_sample_weights(ws, bs, B, T):
    corner = ws[:, :T, :T]
    ws_s = jnp.einsum('bc,gts->gbtcs', jnp.eye(B, dtype=ws.dtype), corner)
    return ws_s.reshape(SGU_GROUPS, B * T, B * T), jnp.tile(_sgu_bias(bs)[:T], (B, 1))


def _sgu(h, g, win, lng, lnb, ws, bias, tm, emit_v):
    B, T, _ = h.shape
    row = lambda b, t: (b, t, 0)
    w = lambda a: pl.BlockSpec(a.shape, lambda b, t: (0,) * a.ndim)
    out_shape = [jax.ShapeDtypeStruct((B, T, D_MODEL), BF16)]
    out_specs = [pl.BlockSpec((None, tm, D_MODEL), row)]
    if emit_v:
        out_shape.append(jax.ShapeDtypeStruct((B, T, D_MODEL), F32))
        out_specs.append(pl.BlockSpec((None, tm, D_MODEL), row))
    return pl.pallas_call(
        functools.partial(_sgu_kernel, emit_v=emit_v),
        grid=(B, T // tm),
        in_specs=[pl.BlockSpec((None, tm, D_MODEL), row), w(g), w(win), w(lng), w(lnb), w(ws), w(bias)],
        out_specs=out_specs,
        out_shape=out_shape,
        compiler_params=_cparams(2),
        name="sgu",
    )(h, g, win, lng, lnb, ws, bias)


def _pad_axis(a, axis, size):
    pad = [(0, 0)] * a.ndim
    pad[axis] = (0, size - a.shape[axis])
    return jnp.pad(a, pad)


def _sample_attention(qs, ksb, vsb, qis, kws, kisb, page_table, cache_k, cache_v, cache_ki, Bs, Ts):
    Ns = Bs * Ts
    n_pool = cache_k.shape[1]
    past = page_table.shape[1] * PAGE_SIZE
    qi_pad = _pad_axis(qis.reshape(Bs, Ts, N_IDX_HEADS, IDX_DIM).transpose(0, 2, 1, 3), 2, SUBLANES)
    qi_pad = qi_pad.reshape(Bs, N_IDX_HEADS * SUBLANES, IDX_DIM)
    wcol = kws[0, :, IDX_DIM:IDX_DIM + N_IDX_HEADS].reshape(Bs, Ts, N_IDX_HEADS).transpose(0, 2, 1)
    wcol = _pad_axis(wcol, 2, SUBLANES).reshape(Bs, N_IDX_HEADS * SUBLANES, 1)
    wcol = jnp.broadcast_to(wcol, (Bs, N_IDX_HEADS * SUBLANES, LANES))
    kinew_t = jnp.swapaxes(_pad_axis(kisb.reshape(Bs, Ts, IDX_DIM), 1, PAGE_SIZE), 1, 2)
    cache_ki_t = jnp.swapaxes(cache_ki, 2, 3).reshape(n_pool, IDX_DIM, PAGE_SIZE)
    sc = _score_sample(page_table, qi_pad, wcol, cache_ki_t, kinew_t)
    n_sel = min(TOPK_MAX, (past + Ts) // 4)
    sel = _select_sample(sc.reshape(Bs * SUBLANES, past + SUPER), n_sel)
    sel = sel.reshape(Bs, SUBLANES, past + SUPER)
    q_all = qs.reshape(Bs, Ts, N_HEADS, HEAD_DIM).transpose(0, 2, 1, 3)
    q_all = _pad_axis(q_all, 2, SUBLANES).reshape(Bs, N_HEADS * SUBLANES, HEAD_DIM)
    pw = N_KV_HEADS * PAGE_SIZE
    knew = _pad_axis(ksb.reshape(Bs, Ts * N_KV_HEADS, HEAD_DIM), 1, pw)
    vnew = _pad_axis(vsb.reshape(Bs, Ts * N_KV_HEADS, HEAD_DIM), 1, pw)
    o_s = _attn_sample(page_table, q_all, sel, cache_k.reshape(n_pool * pw, HEAD_DIM),
                       cache_v.reshape(n_pool * pw, HEAD_DIM), knew, vnew)
    o_s = o_s.reshape(Bs, N_HEADS, SUBLANES, HEAD_DIM)[:, :, :Ts]
    return o_s.transpose(0, 2, 1, 3).reshape(1, Ns, Q_W).astype(BF16)


def kernel(x_prompt, x_sample, cache_k, cache_v, cache_idx_k, state_ffn_conv, page_table, attn_norm,
           w_attn_in, w_attn_out, sgu_norm, w_sgu_in, sgu_ln_g, sgu_ln_b, sgu_ws, sgu_bs, w_sgu_out,
           ffn_norm, w_ffn_up, ffn_conv_w, ffn_conv_b, w_ffn_down, final_norm):
    B, S, _ = x_prompt.shape
    Bs, Ts, _ = x_sample.shape
    Ns = Bs * Ts
    n_pages = page_table.shape[1]
    past = n_pages * PAGE_SIZE
    assert w_ffn_up.shape[0] == 2 and w_attn_in.shape[0] == 1 and w_sgu_in.shape[0] == 1
    assert Ns == CHUNK and Ts == 4 and S % SUPER == 0 and n_pages % PAGES_PER_STEP == 0
    bf = lambda a: a.astype(BF16)
    r1 = lambda a: a[None]
    tm_p, tm_s = 512, Ns

    w_in = bf(_pad_axis(w_attn_in[0], 1, ATTN_IN_PAD))
    g_attn = r1(attn_norm[0])
    q, k32, v32, kb, vb, qi, kw, kib = _attn_in(x_prompt, jnp.arange(S), g_attn, w_in, tm_p)
    wi_t = jnp.swapaxes(kw[..., IDX_DIM:IDX_DIM + N_IDX_HEADS], 1, 2)
    o_p = _attn_prompt(q, kb, vb, qi, kib, wi_t)

    xs = x_sample.reshape(1, Ns, D_MODEL)
    pos_s = past + (jnp.arange(Ns) % Ts)
    qs, ks32, vs32, ksb, vsb, qis, kws, kisb = _attn_in(xs, pos_s, g_attn, w_in, tm_s)
    o_s = _sample_attention(qs, ksb, vsb, qis, kws, kisb, page_table,
                            cache_k[0:1], cache_v[0:1], cache_idx_k[0:1], Bs, Ts)

    def ffn(i, h, o, wout, gfin, sample):
        state = None
        if sample:
            st = _ffn_cols(state_ffn_conv[i])
            state = (jnp.repeat(st[:, 1], Ts, axis=0), jnp.repeat(st[:, 0], Ts, axis=0))
        return _ffn(h, o, bf(wout), r1(ffn_norm[i]), bf(_ffn_cols(w_ffn_up[i])),
                    _ffn_cols(ffn_conv_w[i]), r1(_ffn_cols(ffn_conv_b[i])), bf(w_ffn_down[i]),
                    gfin, state, tm_s if sample else tm_p)

    hp, cst_p0 = ffn(0, x_prompt, o_p, w_attn_out[0], None, False)
    hs, up_s0 = ffn(0, xs, o_s, w_attn_out[0], None, True)

    sgu_args = (r1(sgu_norm[0]), bf(w_sgu_in[0]), r1(sgu_ln_g[0]), r1(sgu_ln_b[0]))
    (y_p,) = _sgu(hp, *sgu_args, sgu_ws[0], _sgu_bias(sgu_bs[0]), tm_p, False)
    ws_s, bias_s = _sgu_sample_weights(sgu_ws[0], sgu_bs[0], Bs, Ts)
    y_s, vn_s = _sgu(hs, *sgu_args, ws_s, bias_s, tm_s, True)

    gfin = r1(final_norm)
    yp, cst_p1 = ffn(1, hp, y_p, w_sgu_out[0], gfin, False)
    ys, up_s1 = ffn(1, hs, y_s, w_sgu_out[0], gfin, True)

    tail = lambda c: _ffn_cols_inv(c[:, SUBLANES - (CONV_W - 1):])
    stail = lambda u: _ffn_cols_inv(u.reshape(Bs, Ts, 2 * D_FF)[:, Ts - (CONV_W - 1):])
    return (yp, ys.reshape(Bs, Ts, D_MODEL),
            k32.reshape(1, B, S, N_KV_HEADS, HEAD_DIM), v32.reshape(1, B, S, N_KV_HEADS, HEAD_DIM),
            kw[None, ..., :IDX_DIM],
            ks32.reshape(1, Bs, Ts, N_KV_HEADS, HEAD_DIM), vs32.reshape(1, Bs, Ts, N_KV_HEADS, HEAD_DIM),
            kws[..., :IDX_DIM].reshape(1, Bs, Ts, IDX_DIM),
            vn_s.reshape(1, Bs, Ts, D_MODEL),
            jnp.stack([tail(cst_p0), tail(cst_p1)]),
            jnp.stack([stail(up_s0), stail(up_s1)]))
```

```python
import functools

import numpy as np
import jax
import jax.numpy as jnp
from jax import lax
from jax.experimental import pallas as pl
from jax.experimental.pallas import tpu as pltpu

F32 = jnp.float32
BF16 = jnp.bfloat16

D_MODEL = 1024
N_HEADS = 8
HEAD_DIM = 128
N_KV_HEADS = 2
HEADS_PER_KV = N_HEADS // N_KV_HEADS
ROT_DIM = 32
ROPE_THETA = 500000.0
N_IDX_HEADS = 4
IDX_DIM = 64
IDX_ROT_DIM = 16
TOPK_MAX = 256
PAGE_SIZE = 128
Q_W = N_HEADS * HEAD_DIM
KV_W = N_KV_HEADS * HEAD_DIM
QI_W = N_IDX_HEADS * IDX_DIM
ATTN_IN = Q_W + 2 * KV_W + QI_W + IDX_DIM + N_IDX_HEADS
ATTN_IN_PAD = 1920
KW_COL = Q_W + 2 * KV_W + QI_W
CHUNK = 128
SGU_GROUPS = 4
SGU_GROUP_DIM = D_MODEL // SGU_GROUPS
D_FF = 2816
CONV_W = 3
NORM_EPS = 1e-6
LN_EPS = 1e-5
LOG2E = 1.4426950408889634
ATTN_SCALE_LOG2 = HEAD_DIM ** -0.5 * LOG2E
W_IDX_SCALE = (N_IDX_HEADS ** -0.5) * (IDX_DIM ** -0.5)

LANES = 128
SUBLANES = 8
KB = 128
SUPER = 512
ATTN_ROW_PARTS = 2
RANK_BLOCK = 256
FF_CHUNK = 256
N_FF_CHUNKS = D_FF // FF_CHUNK
PAGES_PER_STEP = 16
SAMPLE_CHAINS = 2
SEL_ROWS = 64
N_BISECT = 20
VMEM_LIMIT = 56 * 1024 * 1024
MASK_BIAS = -1e30

_NT = (((1,), (1,)), ((), ()))


def _cparams(n_axes):
    return pltpu.CompilerParams(
        dimension_semantics=("arbitrary",) * n_axes, vmem_limit_bytes=VMEM_LIMIT)


def _rmsnorm(x, g):
    return x * lax.rsqrt(jnp.mean(x * x, axis=-1, keepdims=True) + NORM_EPS) * g


def _rope(u, c, sa, sb, half):
    return u * c + pltpu.roll(u, LANES - half, 1) * sa + pltpu.roll(u, half, 1) * sb


def _attn_in_kernel(x_ref, g_ref, w_ref, ch_ref, sha_ref, shb_ref, ci_ref, sia_ref, sib_ref,
                    ckw_ref, skwa_ref, skwb_ref,
                    q_ref, k_ref, v_ref, kb_ref, vb_ref, qi_ref, kw_ref, kib_ref):
    xn = _rmsnorm(x_ref[...], g_ref[...]).astype(BF16)
    z = jnp.dot(xn, w_ref[...], preferred_element_type=F32)
    ch, sha, shb = ch_ref[...], sha_ref[...], shb_ref[...]
    for h in range(N_HEADS):
        sl = slice(h * HEAD_DIM, (h + 1) * HEAD_DIM)
        q_ref[:, sl] = _rope(z[:, sl], ch, sha, shb, ROT_DIM // 2).astype(BF16)
    for h in range(N_KV_HEADS):
        sl = slice(h * HEAD_DIM, (h + 1) * HEAD_DIM)
        kk = _rope(z[:, Q_W + h * HEAD_DIM:Q_W + (h + 1) * HEAD_DIM], ch, sha, shb, ROT_DIM // 2)
        k_ref[pl.ds(h, kk.shape[0], stride=N_KV_HEADS), :] = kk
        kb_ref[:, sl] = kk.astype(BF16)
    v = z[:, Q_W + KV_W:Q_W + 2 * KV_W]
    for h in range(N_KV_HEADS):
        v_ref[pl.ds(h, v.shape[0], stride=N_KV_HEADS), :] = v[:, h * HEAD_DIM:(h + 1) * HEAD_DIM]
    vb_ref[...] = v.astype(BF16)
    ci, sia, sib = ci_ref[...], sia_ref[...], sib_ref[...]
    qi0 = Q_W + 2 * KV_W
    for s in range(QI_W // LANES):
        sl = slice(s * LANES, (s + 1) * LANES)
        qi_ref[:, sl] = _rope(z[:, qi0 + s * LANES:qi0 + (s + 1) * LANES], ci, sia, sib,
                              IDX_ROT_DIM // 2).astype(BF16)
    kw = _rope(z[:, KW_COL:KW_COL + LANES], ckw_ref[...], skwa_ref[...], skwb_ref[...],
               IDX_ROT_DIM // 2)
    kw_ref[...] = kw
    kib_ref[...] = kw[:, :IDX_DIM].astype(BF16)


def _rope_tables(pos, rot, period, tail_scale=None):
    half = rot // 2
    freqs = ROPE_THETA ** (-(jnp.arange(half, dtype=F32) * 2.0) / rot)
    ang = pos.astype(F32)[:, None] * freqs[None, :]
    cos, sin = jnp.cos(ang), jnp.sin(ang)
    lane = np.arange(LANES) % period
    col = lane % half
    c = jnp.where(lane < rot, cos[:, col], 1.0)
    sa = jnp.where(lane < half, -sin[:, col], 0.0)
    sb = jnp.where((lane >= half) & (lane < rot), sin[:, col], 0.0)
    if tail_scale is not None:
        glane = np.arange(LANES)
        c = jnp.where(glane < period, c, 1.0)
        c = jnp.where((glane >= IDX_DIM) & (glane < IDX_DIM + N_IDX_HEADS), tail_scale, c)
        sa = jnp.where(glane < period, sa, 0.0)
        sb = jnp.where(glane < period, sb, 0.0)
    return c.astype(F32), sa.astype(F32), sb.astype(F32)


def _attn_in(x, pos, g, w_pad, tm):
    B, T, _ = x.shape
    ch, sha, shb = _rope_tables(pos, ROT_DIM, HEAD_DIM)
    ci, sia, sib = _rope_tables(pos, IDX_ROT_DIM, IDX_DIM)
    ckw, skwa, skwb = _rope_tables(pos, IDX_ROT_DIM, IDX_DIM, tail_scale=W_IDX_SCALE)
    row = lambda b, t: (b, t, 0)
    tab = lambda b, t: (t, 0)
    const = lambda b, t: (0, 0)
    blk = lambda w: pl.BlockSpec((None, tm, w), row)
    tspec = pl.BlockSpec((tm, LANES), tab)
    kvrows = pl.BlockSpec((None, tm * N_KV_HEADS, HEAD_DIM), row)
    outs = pl.pallas_call(
        _attn_in_kernel,
        grid=(B, T // tm),
        in_specs=[blk(D_MODEL), pl.BlockSpec((1, D_MODEL), const),
                  pl.BlockSpec((D_MODEL, ATTN_IN_PAD), const),
                  tspec, tspec, tspec, tspec, tspec, tspec, tspec, tspec, tspec],
        out_specs=[blk(Q_W), kvrows, kvrows, blk(KV_W), blk(KV_W), blk(QI_W),
                   blk(LANES), blk(IDX_DIM)],
        out_shape=[jax.ShapeDtypeStruct((B, T, Q_W), BF16),
                   jax.ShapeDtypeStruct((B, T * N_KV_HEADS, HEAD_DIM), F32),
                   jax.ShapeDtypeStruct((B, T * N_KV_HEADS, HEAD_DIM), F32),
                   jax.ShapeDtypeStruct((B, T, KV_W), BF16),
                   jax.ShapeDtypeStruct((B, T, KV_W), BF16),
                   jax.ShapeDtypeStruct((B, T, QI_W), BF16),
                   jax.ShapeDtypeStruct((B, T, LANES), F32),
                   jax.ShapeDtypeStruct((B, T, IDX_DIM), BF16)],
        compiler_params=_cparams(2),
        name="attn_in",
    )(x, g, w_pad, ch, sha, shb, ci, sia, sib, ckw, skwa, skwb)
    return outs


def _search_threshold(count_ge, bracket, scan, lo0, hi0, n0, kf):
    hi0 = jnp.where(n0 <= kf, lo0, hi0)

    def coarse(_, st):
        lo, hi = st
        p = 0.5 * lo + 0.5 * hi
        ge = count_ge(p) >= kf
        return jnp.where(ge, p, lo), jnp.where(ge, hi, p)

    lo, hi = lax.fori_loop(0, N_BISECT, coarse, (lo0, hi0))
    a, b = bracket(lo, hi)

    def n_open(a, b):
        return jnp.max(jnp.where(a < b, 1, 0).astype(jnp.int32))

    def body(carry):
        _, a, b = carry
        p = jnp.minimum(jnp.maximum(0.5 * a + 0.5 * b, a), b)
        p = jnp.where(p > a, p, b)
        cnt, lowmax, highmin = scan(p)
        act = a < b
        a2 = jnp.where(act, jnp.where(cnt >= kf, highmin, a), a)
        b2 = jnp.where(act, jnp.where(cnt >= kf, b, lowmax), b)
        return n_open(a2, b2), a2, b2

    out = lax.while_loop(lambda c: c[0] > 0, body, (n_open(a, b), a, b))
    return out[1]


ACC_ROWS = 64


def _rows_reduce8(x, op):
    return op(x.reshape(x.shape[0] // ACC_ROWS, ACC_ROWS, LANES), axis=0)


def _attn_prompt_kernel(q_ref, kb_ref, vb_ref, qi_ref, kib_ref, wi_ref, o_ref,
                        sc_ref, bias_ref, vext_ref, tri_ref, m_ref, acc_ref, *, n_sel):
    i = pl.program_id(1)
    nsup = i // (SUPER // KB) + 1
    kf = float(n_sel)
    inf = jnp.float32(jnp.inf)

    @pl.when(i == 0)
    def _():
        for g in range(N_KV_HEADS):
            vext_ref[g, :, :HEAD_DIM] = vb_ref[:, g * HEAD_DIM:(g + 1) * HEAD_DIM]
            vext_ref[g, :, HEAD_DIM:] = jnp.ones((vext_ref.shape[1], HEAD_DIM), BF16)
        r = lax.broadcasted_iota(jnp.int32, (RANK_BLOCK, RANK_BLOCK), 0)
        c = lax.broadcasted_iota(jnp.int32, (RANK_BLOCK, RANK_BLOCK), 1)
        tri_ref[...] = jnp.where(c <= r, 1.0, 0.0).astype(BF16)

    def rows(u):
        return pl.ds(pl.multiple_of(u * SUPER, SUPER), SUPER)

    def sweep(fn, init):
        return lax.fori_loop(0, nsup, lambda u, c: fn(sc_ref[rows(u), :], c), init)

    z8 = jnp.zeros((ACC_ROWS, LANES), F32)
    fold = lambda acc8, op: op(acc8, axis=0, keepdims=True)

    w = wi_ref[...]
    qi = qi_ref[...]
    qi_rows = jnp.concatenate([qi[:, h * IDX_DIM:(h + 1) * IDX_DIM] for h in range(N_IDX_HEADS)],
                              axis=0)
    kmq = (lax.broadcasted_iota(jnp.int32, (SUPER, KB), 0)
           - lax.broadcasted_iota(jnp.int32, (SUPER, KB), 1))

    def p1(u, carry):
        amin, amax = carry
        kic = kib_ref[rows(u), :]
        sh = lax.dot_general(kic, qi_rows, _NT, preferred_element_type=F32)
        s = None
        for h in range(N_IDX_HEADS):
            t = w[h:h + 1, :] * jnp.maximum(sh[:, h * KB:(h + 1) * KB], 0.0)
            s = t if s is None else s + t
        causal = kmq <= i * KB - u * SUPER
        x = jnp.where(causal, s, -inf)
        sc_ref[rows(u), :] = x
        amin = jnp.minimum(amin, _rows_reduce8(jnp.where(causal, s, inf), jnp.min))
        amax = jnp.maximum(amax, _rows_reduce8(x, jnp.max))
        return amin, amax

    amin8, amax8 = lax.fori_loop(0, nsup, p1, (z8 + inf, z8 - inf))
    lane = lax.broadcasted_iota(jnp.int32, (1, LANES), 1)
    n0 = (i * KB + lane + 1).astype(F32)

    def count_ge(p):
        return fold(sweep(lambda x, c: c + _rows_reduce8(jnp.where(x >= p, 1.0, 0.0), jnp.sum), z8),
                    jnp.sum)

    def bracket(lo, hi):
        def f(x, c):
            return (jnp.minimum(c[0], _rows_reduce8(jnp.where(x >= lo, x, inf), jnp.min)),
                    jnp.maximum(c[1], _rows_reduce8(jnp.where(x <= hi, x, -inf), jnp.max)))
        a8, b8 = sweep(f, (z8 + inf, z8 - inf))
        return fold(a8, jnp.min), fold(b8, jnp.max)

    def scan(p):
        def f(x, c):
            ge = x >= p
            return (c[0] + _rows_reduce8(jnp.where(ge, 1.0, 0.0), jnp.sum),
                    jnp.maximum(c[1], _rows_reduce8(jnp.where(ge, -inf, x), jnp.max)),
                    jnp.minimum(c[2], _rows_reduce8(jnp.where(ge, x, inf), jnp.min)))
        cnt, lo, hi = sweep(f, (z8, z8 - inf, z8 + inf))
        return fold(cnt, jnp.sum), fold(lo, jnp.max), fold(hi, jnp.min)

    thr = _search_threshold(count_ge, bracket, scan, fold(amin8, jnp.min), fold(amax8, jnp.max),
                            n0, kf)

    cgt = fold(sweep(lambda x, c: c + _rows_reduce8(jnp.where(x > thr, 1.0, 0.0), jnp.sum), z8),
               jnp.sum)
    need = kf - cgt

    tri = tri_ref[...]

    def p3(u, run):
        for part in range(SUPER // RANK_BLOCK):
            rs = pl.ds(pl.multiple_of(u * SUPER + part * RANK_BLOCK, RANK_BLOCK), RANK_BLOCK)
            x = sc_ref[rs, :]
            eq = x == thr
            rank = jnp.dot(tri, jnp.where(eq, 1.0, 0.0).astype(BF16),
                           preferred_element_type=F32) + run
            keep_tie = jnp.where(rank <= need, 0.0, MASK_BIAS)
            bias = jnp.where(x > thr, 0.0, jnp.where(eq, keep_tie, MASK_BIAS))
            bias_ref[rs, :] = bias.astype(BF16)
            run = rank[RANK_BLOCK - 1:RANK_BLOCK, :]
        return run

    lax.fori_loop(0, nsup, p3, jnp.zeros((1, LANES), F32))

    eye = jnp.where(lax.broadcasted_iota(jnp.int32, (KB, KB), 0)
                    == lax.broadcasted_iota(jnp.int32, (KB, KB), 1), 1.0, 0.0).astype(BF16)
    eye4 = jnp.concatenate([eye] * HEADS_PER_KV, axis=0)
    ags = []
    for g in range(N_KV_HEADS):
        qg = jnp.concatenate(
            [q_ref[:, (g * HEADS_PER_KV + r) * HEAD_DIM:(g * HEADS_PER_KV + r + 1) * HEAD_DIM]
             for r in range(HEADS_PER_KV)], axis=0)
        ags.append(jnp.concatenate([qg, eye4], axis=1))
    m_ref[...] = jnp.full(m_ref.shape, -inf, F32)
    acc_ref[...] = jnp.zeros(acc_ref.shape, F32)
    nrow = HEADS_PER_KV * KB
    part = nrow // ATTN_ROW_PARTS

    def pa(u, c):
        bias_t = bias_ref[rows(u), :]
        for g in range(N_KV_HEADS):
            bmat = jnp.concatenate([kb_ref[rows(u), g * HEAD_DIM:(g + 1) * HEAD_DIM], bias_t],
                                   axis=1)
            vx = vext_ref[g, rows(u), :]
            for hp in range(ATTN_ROW_PARTS):
                rs = slice(hp * part, (hp + 1) * part)
                t = lax.dot_general(ags[g][rs], bmat, _NT,
                                    preferred_element_type=F32) * ATTN_SCALE_LOG2
                m_old = m_ref[g, rs]
                m_new = jnp.maximum(m_old, jnp.max(t, axis=1, keepdims=True))
                p = jnp.exp2(t - m_new).astype(BF16)
                acc_ref[g, rs] = (jnp.exp2(m_old - m_new) * acc_ref[g, rs]
                                  + jnp.dot(p, vx, preferred_element_type=F32))
                m_ref[g, rs] = m_new
        return c

    lax.fori_loop(0, nsup, pa, 0)
    for g in range(N_KV_HEADS):
        acc = acc_ref[g]
        o = acc[:, :HEAD_DIM] / acc[:, HEAD_DIM:]
        for r in range(HEADS_PER_KV):
            h = g * HEADS_PER_KV + r
            o_ref[:, h * HEAD_DIM:(h + 1) * HEAD_DIM] = o[r * KB:(r + 1) * KB].astype(BF16)


def _attn_prompt(q, kb, vb, qi, kib, wi_t):
    B, S, _ = q.shape
    n_sel = min(TOPK_MAX, S // 4)
    qblk = lambda w: pl.BlockSpec((None, KB, w), lambda b, i: (b, i, 0))
    full = lambda w: pl.BlockSpec((None, S, w), lambda b, i: (b, 0, 0))
    nrow = HEADS_PER_KV * KB
    return pl.pallas_call(
        functools.partial(_attn_prompt_kernel, n_sel=n_sel),
        grid=(B, S // KB),
        in_specs=[qblk(Q_W), full(KV_W), full(KV_W), qblk(QI_W), full(IDX_DIM),
                  pl.BlockSpec((None, N_IDX_HEADS, KB), lambda b, i: (b, 0, i))],
        out_specs=qblk(Q_W),
        out_shape=jax.ShapeDtypeStruct((B, S, Q_W), BF16),
        scratch_shapes=[pltpu.VMEM((S, KB), F32), pltpu.VMEM((S, KB), BF16),
                        pltpu.VMEM((N_KV_HEADS, S, 2 * HEAD_DIM), BF16),
                        pltpu.VMEM((RANK_BLOCK, RANK_BLOCK), BF16),
                        pltpu.VMEM((N_KV_HEADS, nrow, 1), F32),
                        pltpu.VMEM((N_KV_HEADS, nrow, 2 * HEAD_DIM), F32)],
        compiler_params=_cparams(2),
        name="attn_prompt",
    )(q, kb, vb, qi, kib, wi_t)


def _idx_scores(qi, wcol, keys_t):
    s = jnp.dot(qi, keys_t, preferred_element_type=F32)
    r = jnp.maximum(s, 0.0) * wcol
    out = r[0:SUBLANES]
    for h in range(1, N_IDX_HEADS):
        out = out + r[h * SUBLANES:(h + 1) * SUBLANES]
    return out


def _score_sample_kernel(pt_ref, qi_ref, w_ref, *refs, past):
    ki_refs = refs[:PAGES_PER_STEP]
    kinew_ref, sc_ref = refs[PAGES_PER_STEP:]
    j = pl.program_id(1)
    qi = qi_ref[...]
    wcol = w_ref[:, 0:1]
    width = PAGES_PER_STEP * PAGE_SIZE
    keys_t = jnp.concatenate([r[...] for r in ki_refs], axis=1).astype(BF16)
    sc_ref[:, pl.ds(pl.multiple_of(j * width, width), width)] = _idx_scores(qi, wcol, keys_t)

    @pl.when(j == pl.num_programs(1) - 1)
    def _():
        inf = jnp.float32(jnp.inf)
        row = lax.broadcasted_iota(jnp.int32, (SUBLANES, PAGE_SIZE), 0)
        lane = lax.broadcasted_iota(jnp.int32, (SUBLANES, PAGE_SIZE), 1)
        snew = _idx_scores(qi, wcol, kinew_ref[...])
        sc_ref[:, past:past + PAGE_SIZE] = jnp.where(lane <= row, snew, -inf)
        sc_ref[:, past + PAGE_SIZE:] = jnp.full((SUBLANES, SUPER - PAGE_SIZE), -inf, F32)


def _score_sample(page_table, qi_pad, wcol, cache_ki_t, kinew_t):
    B, n_pages = page_table.shape
    past = n_pages * PAGE_SIZE
    steps = n_pages // PAGES_PER_STEP
    page = lambda p: pl.BlockSpec((None, IDX_DIM, PAGE_SIZE),
                                  lambda b, j, pt: (pt[b, j * PAGES_PER_STEP + p], 0, 0))
    per_b = lambda r, w: pl.BlockSpec((None, r, w), lambda b, j, pt: (b, 0, 0))
    return pl.pallas_call(
        functools.partial(_score_sample_kernel, past=past),
        grid_spec=pltpu.PrefetchScalarGridSpec(
            num_scalar_prefetch=1,
            grid=(B, steps),
            in_specs=[per_b(N_IDX_HEADS * SUBLANES, IDX_DIM), per_b(N_IDX_HEADS * SUBLANES, LANES)]
                     + [page(p) for p in range(PAGES_PER_STEP)]
                     + [per_b(IDX_DIM, PAGE_SIZE)],
            out_specs=per_b(SUBLANES, past + SUPER),
        ),
        out_shape=jax.ShapeDtypeStruct((B, SUBLANES, past + SUPER), F32),
        compiler_params=_cparams(2),
        name="score_sample",
    )(page_table, qi_pad, wcol, *([cache_ki_t] * PAGES_PER_STEP), kinew_t)


def _lanes_reduce(x, op):
    out = x[:, :LANES]
    for t in range(1, x.shape[1] // LANES):
        out = op(out, x[:, t * LANES:(t + 1) * LANES])
    return out


def _select_sample_kernel(sc_ref, sel_ref, tri_ref, spread_ref, *, n_sel):
    R, L = sc_ref.shape
    nsup = L // SUPER
    kf = float(n_sel)
    inf = jnp.float32(jnp.inf)
    r = lax.broadcasted_iota(jnp.int32, (SUPER, SUPER), 0)
    c = lax.broadcasted_iota(jnp.int32, (SUPER, SUPER), 1)
    tri_ref[...] = jnp.where(r <= c, 1.0, 0.0).astype(BF16)

    def cols(u):
        return pl.ds(pl.multiple_of(u * SUPER, SUPER), SUPER)

    def sweep(fn, init):
        return lax.fori_loop(0, nsup, lambda u, cr: fn(sc_ref[:, cols(u)], cr), init)

    zr = jnp.zeros((R, LANES), F32)
    rsum = lambda a: jnp.sum(a, axis=1, keepdims=True)
    rmax = lambda a: jnp.max(a, axis=1, keepdims=True)
    rmin = lambda a: jnp.min(a, axis=1, keepdims=True)
    one = lambda m: jnp.where(m, 1.0, 0.0)

    def f0(x, cr):
        fin = x > -inf
        return (jnp.minimum(cr[0], _lanes_reduce(jnp.where(fin, x, inf), jnp.minimum)),
                jnp.maximum(cr[1], _lanes_reduce(x, jnp.maximum)),
                cr[2] + _lanes_reduce(one(fin), jnp.add))
    lo8, hi8, n8 = sweep(f0, (zr + inf, zr - inf, zr))

    def count_ge(p):
        return rsum(sweep(lambda x, cr: cr + _lanes_reduce(one(x >= p), jnp.add), zr))

    def bracket(lo, hi):
        def f(x, cr):
            return (jnp.minimum(cr[0], _lanes_reduce(jnp.where(x >= lo, x, inf), jnp.minimum)),
                    jnp.maximum(cr[1], _lanes_reduce(jnp.where(x <= hi, x, -inf), jnp.maximum)))
        a8, b8 = sweep(f, (zr + inf, zr - inf))
        return rmin(a8), rmax(b8)

    def scan(p):
        def f(x, cr):
            ge = x >= p
            return (cr[0] + _lanes_reduce(one(ge), jnp.add),
                    jnp.maximum(cr[1], _lanes_reduce(jnp.where(ge, -inf, x), jnp.maximum)),
                    jnp.minimum(cr[2], _lanes_reduce(jnp.where(ge, x, inf), jnp.minimum)))
        cnt, lo, hi = sweep(f, (zr, zr - inf, zr + inf))
        return rsum(cnt), rmax(lo), rmin(hi)

    thr = _search_threshold(count_ge, bracket, scan, rmin(lo8), rmax(hi8), rsum(n8), kf)
    need = kf - rsum(sweep(lambda x, cr: cr + _lanes_reduce(one(x > thr), jnp.add), zr))

    wide = N_KV_HEADS * SUPER
    spread_ref[...] = jnp.where(
        lax.broadcasted_iota(jnp.int32, (SUPER, wide), 1) // N_KV_HEADS
        == lax.broadcasted_iota(jnp.int32, (SUPER, wide), 0), 1.0, 0.0).astype(BF16)

    def p3(u, run):
        x = sc_ref[:, cols(u)]
        eq = x == thr
        rank = jnp.dot(one(eq).astype(BF16), tri_ref[...], preferred_element_type=F32) + run
        sel = jnp.where(x > thr, 1.0, jnp.where(eq, one(rank <= need), 0.0))
        sel_ref[:, pl.ds(pl.multiple_of(u * wide, wide), wide)] = jnp.dot(
            sel.astype(BF16), spread_ref[...], preferred_element_type=F32)
        return rank[:, SUPER - 1:SUPER]

    lax.fori_loop(0, nsup, p3, jnp.zeros((R, 1), F32))


def _select_sample(sc, n_sel):
    R, L = sc.shape
    return pl.pallas_call(
        functools.partial(_select_sample_kernel, n_sel=n_sel),
        grid=(R // SEL_ROWS,),
        in_specs=[pl.BlockSpec((SEL_ROWS, L), lambda i: (i, 0))],
        out_specs=pl.BlockSpec((SEL_ROWS, N_KV_HEADS * L), lambda i: (i, 0)),
        out_shape=jax.ShapeDtypeStruct((R, N_KV_HEADS * L), F32),
        scratch_shapes=[pltpu.VMEM((SUPER, SUPER), BF16),
                        pltpu.VMEM((SUPER, N_KV_HEADS * SUPER), BF16)],
        compiler_params=_cparams(1),
        name="select_sample",
    )(sc)


def _attn_sample_kernel(pt_ref, q_ref, sel_ref, *refs, past):
    k_refs = refs[:PAGES_PER_STEP]
    v_refs = refs[PAGES_PER_STEP:2 * PAGES_PER_STEP]
    knew_ref, vnew_ref, o_ref, m_ref, acc_ref = refs[2 * PAGES_PER_STEP:]
    j = pl.program_id(1)
    nrow = N_HEADS * SUBLANES
    pw = N_KV_HEADS * PAGE_SIZE
    per_chain = PAGES_PER_STEP // SAMPLE_CHAINS

    @pl.when(j == 0)
    def _():
        m_ref[...] = jnp.full(m_ref.shape, -jnp.inf, F32)
        acc_ref[...] = jnp.zeros(acc_ref.shape, F32)

    q = q_ref[...]

    def update(ch, kk, vv, col):
        n = kk.shape[0]
        s = lax.dot_general(q, kk, _NT, preferred_element_type=F32) * ATTN_SCALE_LOG2
        sel = jnp.concatenate([sel_ref[:, pl.ds(col, n)]] * N_HEADS, axis=0)
        own = (lax.broadcasted_iota(jnp.int32, (nrow, n), 0) // (HEADS_PER_KV * SUBLANES)
               == lax.broadcasted_iota(jnp.int32, (nrow, n), 1) % N_KV_HEADS)
        s = jnp.where(own, jnp.where(sel > 0.5, s, MASK_BIAS), MASK_BIAS)
        m_old = m_ref[ch]
        m_new = jnp.maximum(m_old, jnp.max(s, axis=1, keepdims=True))
        pb = jnp.exp2(s - m_new).astype(BF16)
        vext = jnp.concatenate([vv, jnp.ones((n, HEAD_DIM), BF16)], axis=1)
        acc_ref[ch] = (jnp.exp2(m_old - m_new) * acc_ref[ch]
                       + jnp.dot(pb, vext, preferred_element_type=F32))
        m_ref[ch] = m_new

    for ch in range(SAMPLE_CHAINS):
        pages = range(ch * per_chain, (ch + 1) * per_chain)
        kk = jnp.concatenate([k_refs[p][...].astype(BF16) for p in pages], axis=0)
        vv = jnp.concatenate([v_refs[p][...].astype(BF16) for p in pages], axis=0)
        col = pl.multiple_of((j * PAGES_PER_STEP + ch * per_chain) * pw, per_chain * pw)
        update(ch, kk, vv, col)

    @pl.when(j == pl.num_programs(1) - 1)
    def _():
        update(0, knew_ref[...], vnew_ref[...], N_KV_HEADS * past)
        m = m_ref[0]
        for ch in range(1, SAMPLE_CHAINS):
            m = jnp.maximum(m, m_ref[ch])
        acc = jnp.exp2(m_ref[0] - m) * acc_ref[0]
        for ch in range(1, SAMPLE_CHAINS):
            acc = acc + jnp.exp2(m_ref[ch] - m) * acc_ref[ch]
        o_ref[...] = acc[:, :HEAD_DIM] / acc[:, HEAD_DIM:]


def _attn_sample(page_table, q_all, sel, cache_k, cache_v, knew, vnew):
    B, n_pages = page_table.shape
    past = n_pages * PAGE_SIZE
    steps = n_pages // PAGES_PER_STEP
    nrow = N_HEADS * SUBLANES
    pw = N_KV_HEADS * PAGE_SIZE
    page = lambda p: pl.BlockSpec((pw, HEAD_DIM), lambda b, j, pt: (pt[b, j * PAGES_PER_STEP + p], 0))
    per_b = lambda r, w: pl.BlockSpec((None, r, w), lambda b, j, pt: (b, 0, 0))
    return pl.pallas_call(
        functools.partial(_attn_sample_kernel, past=past),
        grid_spec=pltpu.PrefetchScalarGridSpec(
            num_scalar_prefetch=1,
            grid=(B, steps),
            in_specs=[per_b(nrow, HEAD_DIM), per_b(SUBLANES, sel.shape[2])]
                     + [page(p) for p in range(PAGES_PER_STEP)] * 2
                     + [per_b(pw, HEAD_DIM), per_b(pw, HEAD_DIM)],
            out_specs=per_b(nrow, HEAD_DIM),
            scratch_shapes=[pltpu.VMEM((SAMPLE_CHAINS, nrow, 1), F32),
                            pltpu.VMEM((SAMPLE_CHAINS, nrow, 2 * HEAD_DIM), F32)],
        ),
        out_shape=jax.ShapeDtypeStruct((B, nrow, HEAD_DIM), F32),
        compiler_params=_cparams(2),
        name="attn_sample",
    )(page_table, q_all, sel, *([cache_k] * PAGES_PER_STEP), *([cache_v] * PAGES_PER_STEP), knew, vnew)


def _ffn_kernel(*refs, sample, final):
    it = iter(refs)
    h_ref, o_ref, wout_ref, g_ref, wup_ref, cw_ref, cb_ref, wdown_ref = (next(it) for _ in range(8))
    gfin_ref = next(it) if final else None
    st1_ref, st2_ref = (next(it), next(it)) if sample else (None, None)
    hout_ref, cst_ref = next(it), next(it)
    act_ref = next(it)
    carry_ref = None if sample else next(it)

    tm = h_ref.shape[0]
    h1 = h_ref[...] + jnp.dot(o_ref[...], wout_ref[...], preferred_element_type=F32)
    xn = _rmsnorm(h1, g_ref[...]).astype(BF16)

    if sample:
        t4 = lax.broadcasted_iota(jnp.int32, (tm, FF_CHUNK), 0) % 4
    else:
        @pl.when(pl.program_id(1) == 0)
        def _():
            carry_ref[...] = jnp.zeros(carry_ref.shape, F32)
        row8 = lax.broadcasted_iota(jnp.int32, (SUBLANES, FF_CHUNK), 0)

    def conv(cols):
        u = jnp.dot(xn, wup_ref[:, cols], preferred_element_type=F32)
        r1 = pltpu.roll(u, 1, 0)
        r2 = pltpu.roll(u, 2, 0)
        if sample:
            s1 = jnp.where(t4 == 0, st1_ref[:, cols], r1)
            s2 = jnp.where(t4 == 0, st2_ref[:, cols], jnp.where(t4 == 1, st1_ref[:, cols], r2))
            cst_ref[:, cols] = u
        else:
            prev = carry_ref[:, cols]
            top1 = jnp.where(row8 < 1, pltpu.roll(prev, 1, 0), r1[:SUBLANES])
            top2 = jnp.where(row8 < 2, pltpu.roll(prev, 2, 0), r2[:SUBLANES])
            s1 = jnp.concatenate([top1, r1[SUBLANES:]], axis=0)
            s2 = jnp.concatenate([top2, r2[SUBLANES:]], axis=0)
            carry_ref[:, cols] = u[tm - SUBLANES:]
            cst_ref[:, cols] = u[tm - SUBLANES:]
        cw = cw_ref[:, cols]
        return cb_ref[:, cols] + cw[0:1] * s2 + cw[1:2] * s1 + cw[2:3] * u

    for c in range(N_FF_CHUNKS):
        gate = conv(slice(c * FF_CHUNK, (c + 1) * FF_CHUNK))
        val = conv(slice(D_FF + c * FF_CHUNK, D_FF + (c + 1) * FF_CHUNK))
        act_ref[:, c * FF_CHUNK:(c + 1) * FF_CHUNK] = (
            gate * (1.0 / (1.0 + jnp.exp(-gate))) * val).astype(BF16)

    out = h1 + jnp.dot(act_ref[...], wdown_ref[...], preferred_element_type=F32)
    if final:
        out = _rmsnorm(out, gfin_ref[...])
    hout_ref[...] = out


def _ffn(h, o, wout, g, wup, cw, cb, wdown, gfin, state, tm):
    B, T, _ = h.shape
    sample = state is not None
    final = gfin is not None
    row = lambda b, t: (b, t, 0)
    const = lambda b, t: (0, 0)
    w = lambda a: pl.BlockSpec(a.shape, const, pipeline_mode=pl.Buffered(1))
    operands = [h, o, wout, g, wup, cw, cb, wdown]
    in_specs = [pl.BlockSpec((None, tm, D_MODEL), row), pl.BlockSpec((None, tm, D_MODEL), row),
                w(wout), w(g), w(wup), w(cw), w(cb), w(wdown)]
    if final:
        operands.append(gfin)
        in_specs.append(w(gfin))
    scratch = [pltpu.VMEM((tm, D_FF), BF16)]
    if sample:
        operands += [state[0], state[1]]
        in_specs += [w(state[0]), w(state[1])]
        cst_shape = jax.ShapeDtypeStruct((B, T, 2 * D_FF), F32)
        cst_spec = pl.BlockSpec((None, tm, 2 * D_FF), row)
    else:
        cst_shape = jax.ShapeDtypeStruct((B, SUBLANES, 2 * D_FF), F32)
        cst_spec = pl.BlockSpec((None, SUBLANES, 2 * D_FF), lambda b, t: (b, 0, 0))
        scratch.append(pltpu.VMEM((SUBLANES, 2 * D_FF), F32))
    return pl.pallas_call(
        functools.partial(_ffn_kernel, sample=sample, final=final),
        grid=(B, T // tm),
        in_specs=in_specs,
        out_specs=[pl.BlockSpec((None, tm, D_MODEL), row), cst_spec],
        out_shape=[jax.ShapeDtypeStruct((B, T, D_MODEL), F32), cst_shape],
        scratch_shapes=scratch,
        compiler_params=_cparams(2),
        name="ffn_sample" if sample else "ffn_prompt",
    )(*operands)


def _sgu_kernel(h_ref, g_ref, win_ref, lng_ref, lnb_ref, ws_ref, bias_ref, *out_refs, emit_v):
    y_ref = out_refs[0]
    tm = h_ref.shape[0]
    xn = _rmsnorm(h_ref[...], g_ref[...]).astype(BF16)
    uv = jnp.dot(xn, win_ref[...], preferred_element_type=F32)
    u = uv[:, :D_MODEL]
    v = uv[:, D_MODEL:]
    xc = v - jnp.mean(v, axis=-1, keepdims=True)
    vn = xc * lax.rsqrt(jnp.mean(xc * xc, axis=-1, keepdims=True) + LN_EPS) * lng_ref[...] + lnb_ref[...]
    if emit_v:
        out_refs[1][...] = vn
    vnb = vn.astype(BF16)
    tri = (lax.broadcasted_iota(jnp.int32, (CHUNK, CHUNK), 0)
           >= lax.broadcasted_iota(jnp.int32, (CHUNK, CHUNK), 1))
    for g in range(SGU_GROUPS):
        wg = jnp.where(tri, ws_ref[g], 0.0).astype(BF16)
        cols = slice(g * SGU_GROUP_DIM, (g + 1) * SGU_GROUP_DIM)
        for c in range(tm // CHUNK):
            rows = slice(c * CHUNK, (c + 1) * CHUNK)
            s = jnp.dot(wg, vnb[rows, cols], preferred_element_type=F32) + bias_ref[:, cols]
            y_ref[rows, cols] = (u[rows, cols] * s).astype(BF16)


def _sgu_bias(bs):
    return jnp.repeat(bs.T, SGU_GROUP_DIM, axis=1)


def _sgu_sample_weights(ws, bs, B, T):
    corner = ws[:, :T, :T]
    ws_s = jnp.einsum('bc,gts->gbtcs', jnp.eye(B, dtype=ws.dtype), corner)
    return ws_s.reshape(SGU_GROUPS, B * T, B * T), jnp.tile(_sgu_bias(bs)[:T], (B, 1))


def _sgu(h, g, win, lng, lnb, ws, bias, tm, emit_v):
    B, T, _ = h.shape
    row = lambda b, t: (b, t, 0)
    w = lambda a: pl.BlockSpec(a.shape, lambda b, t: (0,) * a.ndim)
    out_shape = [jax.ShapeDtypeStruct((B, T, D_MODEL), BF16)]
    out_specs = [pl.BlockSpec((None, tm, D_MODEL), row)]
    if emit_v:
        out_shape.append(jax.ShapeDtypeStruct((B, T, D_MODEL), F32))
        out_specs.append(pl.BlockSpec((None, tm, D_MODEL), row))
    return pl.pallas_call(
        functools.partial(_sgu_kernel, emit_v=emit_v),
        grid=(B, T // tm),
        in_specs=[pl.BlockSpec((None, tm, D_MODEL), row), w(g), w(win), w(lng), w(lnb), w(ws), w(bias)],
        out_specs=out_specs,
        out_shape=out_shape,
        compiler_params=_cparams(2),
        name="sgu",
    )(h, g, win, lng, lnb, ws, bias)


def _pad_axis(a, axis, size):
    pad = [(0, 0)] * a.ndim
    pad[axis] = (0, size - a.shape[axis])
    return jnp.pad(a, pad)


def _sample_attention(qs, ksb, vsb, qis, kws, kisb, page_table, cache_k, cache_v, cache_ki, Bs, Ts):
    Ns = Bs * Ts
    n_pool = cache_k.shape[1]
    past = page_table.shape[1] * PAGE_SIZE
    qi_pad = _pad_axis(qis.reshape(Bs, Ts, N_IDX_HEADS, IDX_DIM).transpose(0, 2, 1, 3), 2, SUBLANES)
    qi_pad = qi_pad.reshape(Bs, N_IDX_HEADS * SUBLANES, IDX_DIM)
    wcol = kws[0, :, IDX_DIM:IDX_DIM + N_IDX_HEADS].reshape(Bs, Ts, N_IDX_HEADS).transpose(0, 2, 1)
    wcol = _pad_axis(wcol, 2, SUBLANES).reshape(Bs, N_IDX_HEADS * SUBLANES, 1)
    wcol = jnp.broadcast_to(wcol, (Bs, N_IDX_HEADS * SUBLANES, LANES))
    kinew_t = jnp.swapaxes(_pad_axis(kisb.reshape(Bs, Ts, IDX_DIM), 1, PAGE_SIZE), 1, 2)
    cache_ki_t = jnp.swapaxes(cache_ki, 2, 3).reshape(n_pool, IDX_DIM, PAGE_SIZE)
    sc = _score_sample(page_table, qi_pad, wcol, cache_ki_t, kinew_t)
    n_sel = min(TOPK_MAX, (past + Ts) // 4)
    sel = _select_sample(sc.reshape(Bs * SUBLANES, past + SUPER), n_sel)
    sel = sel.reshape(Bs, SUBLANES, N_KV_HEADS * (past + SUPER))
    q_all = qs.reshape(Bs, Ts, N_HEADS, HEAD_DIM).transpose(0, 2, 1, 3)
    q_all = _pad_axis(q_all, 2, SUBLANES).reshape(Bs, N_HEADS * SUBLANES, HEAD_DIM)
    pw = N_KV_HEADS * PAGE_SIZE
    knew = _pad_axis(ksb.reshape(Bs, Ts * N_KV_HEADS, HEAD_DIM), 1, pw)
    vnew = _pad_axis(vsb.reshape(Bs, Ts * N_KV_HEADS, HEAD_DIM), 1, pw)
    o_s = _attn_sample(page_table, q_all, sel, cache_k.reshape(n_pool * pw, HEAD_DIM),
                       cache_v.reshape(n_pool * pw, HEAD_DIM), knew, vnew)
    o_s = o_s.reshape(Bs, N_HEADS, SUBLANES, HEAD_DIM)[:, :, :Ts]
    return o_s.transpose(0, 2, 1, 3).reshape(1, Ns, Q_W).astype(BF16)


def kernel(x_prompt, x_sample, cache_k, cache_v, cache_idx_k, state_ffn_conv, page_table, attn_norm,
           w_attn_in, w_attn_out, sgu_norm, w_sgu_in, sgu_ln_g, sgu_ln_b, sgu_ws, sgu_bs, w_sgu_out,
           ffn_norm, w_ffn_up, ffn_conv_w, ffn_conv_b, w_ffn_down, final_norm):
    B, S, _ = x_prompt.shape
    Bs, Ts, _ = x_sample.shape
    Ns = Bs * Ts
    n_pages = page_table.shape[1]
    past = n_pages * PAGE_SIZE
    assert w_ffn_up.shape[0] == 2 and w_attn_in.shape[0] == 1 and w_sgu_in.shape[0] == 1
    assert Ns == CHUNK and Ts == 4 and S % SUPER == 0 and n_pages % PAGES_PER_STEP == 0
    bf = lambda a: a.astype(BF16)
    r1 = lambda a: a[None]
    tm_p, tm_s = 512, Ns

    w_in = bf(_pad_axis(w_attn_in[0], 1, ATTN_IN_PAD))
    g_attn = r1(attn_norm[0])
    q, k32, v32, kb, vb, qi, kw, kib = _attn_in(x_prompt, jnp.arange(S), g_attn, w_in, tm_p)
    wi_t = jnp.swapaxes(kw[..., IDX_DIM:IDX_DIM + N_IDX_HEADS], 1, 2)
    o_p = _attn_prompt(q, kb, vb, qi, kib, wi_t)

    xs = x_sample.reshape(1, Ns, D_MODEL)
    pos_s = past + (jnp.arange(Ns) % Ts)
    qs, ks32, vs32, ksb, vsb, qis, kws, kisb = _attn_in(xs, pos_s, g_attn, w_in, tm_s)
    o_s = _sample_attention(qs, ksb, vsb, qis, kws, kisb, page_table,
                            cache_k[0:1], cache_v[0:1], cache_idx_k[0:1], Bs, Ts)

    def ffn(i, h, o, wout, gfin, sample):
        state = None
        if sample:
            st = state_ffn_conv[i]
            state = (jnp.repeat(st[:, 1], Ts, axis=0), jnp.repeat(st[:, 0], Ts, axis=0))
        return _ffn(h, o, bf(wout), r1(ffn_norm[i]), bf(w_ffn_up[i]), ffn_conv_w[i],
                    r1(ffn_conv_b[i]), bf(w_ffn_down[i]), gfin, state, tm_s if sample else tm_p)

    hp, cst_p0 = ffn(0, x_prompt, o_p, w_attn_out[0], None, False)
    hs, up_s0 = ffn(0, xs, o_s, w_attn_out[0], None, True)

    sgu_args = (r1(sgu_norm[0]), bf(w_sgu_in[0]), r1(sgu_ln_g[0]), r1(sgu_ln_b[0]))
    (y_p,) = _sgu(hp, *sgu_args, sgu_ws[0], _sgu_bias(sgu_bs[0]), tm_p, False)
    ws_s, bias_s = _sgu_sample_weights(sgu_ws[0], sgu_bs[0], Bs, Ts)
    y_s, vn_s = _sgu(hs, *sgu_args, ws_s, bias_s, tm_s, True)

    gfin = r1(final_norm)
    yp, cst_p1 = ffn(1, hp, y_p, w_sgu_out[0], gfin, False)
    ys, up_s1 = ffn(1, hs, y_s, w_sgu_out[0], gfin, True)

    tail = lambda c: c[:, SUBLANES - (CONV_W - 1):]
    stail = lambda u: u.reshape(Bs, Ts, 2 * D_FF)[:, Ts - (CONV_W - 1):]
    return (yp, ys.reshape(Bs, Ts, D_MODEL),
            k32.reshape(1, B, S, N_KV_HEADS, HEAD_DIM), v32.reshape(1, B, S, N_KV_HEADS, HEAD_DIM),
            kw[None, ..., :IDX_DIM],
            ks32.reshape(1, Bs, Ts, N_KV_HEADS, HEAD_DIM), vs32.reshape(1, Bs, Ts, N_KV_HEADS, HEAD_DIM),
            kws[..., :IDX_DIM].reshape(1, Bs, Ts, IDX_DIM),
            vn_s.reshape(1, Bs, Ts, D_MODEL),
            jnp.stack([tail(cst_p0), tail(cst_p1)]),
            jnp.stack([stail(up_s0), stail(up_s1)]))
```

```python
import functools

import numpy as np
import jax
import jax.numpy as jnp
from jax import lax
from jax.experimental import pallas as pl
from jax.experimental.pallas import tpu as pltpu

F32 = jnp.float32
BF16 = jnp.bfloat16

D_MODEL = 1024
N_HEADS = 8
HEAD_DIM = 128
N_KV_HEADS = 2
HEADS_PER_KV = N_HEADS // N_KV_HEADS
ROT_DIM = 32
ROPE_THETA = 500000.0
N_IDX_HEADS = 4
IDX_DIM = 64
IDX_ROT_DIM = 16
TOPK_MAX = 256
PAGE_SIZE = 128
Q_W = N_HEADS * HEAD_DIM
KV_W = N_KV_HEADS * HEAD_DIM
QI_W = N_IDX_HEADS * IDX_DIM
ATTN_IN = Q_W + 2 * KV_W + QI_W + IDX_DIM + N_IDX_HEADS
ATTN_IN_PAD = 1920
KW_COL = Q_W + 2 * KV_W + QI_W
CHUNK = 128
SGU_GROUPS = 4
SGU_GROUP_DIM = D_MODEL // SGU_GROUPS
D_FF = 2816
CONV_W = 3
NORM_EPS = 1e-6
LN_EPS = 1e-5
LOG2E = 1.4426950408889634
ATTN_SCALE_LOG2 = HEAD_DIM ** -0.5 * LOG2E
W_IDX_SCALE = (N_IDX_HEADS ** -0.5) * (IDX_DIM ** -0.5)

LANES = 128
SUBLANES = 8
KB = 128
SUPER = 512
ATTN_ROW_PARTS = 2
RANK_BLOCK = 256
FF_CHUNK = 256
FFN_ROWS = 1024
N_FF_CHUNKS = D_FF // FF_CHUNK
PAGES_PER_STEP = 16
SAMPLE_CHAINS = 2
SEL_ROWS = 64
N_BISECT = 20
VMEM_LIMIT = 56 * 1024 * 1024
MASK_BIAS = -1e30

_NT = (((1,), (1,)), ((), ()))


def _cparams(n_axes):
    return pltpu.CompilerParams(
        dimension_semantics=("arbitrary",) * n_axes, vmem_limit_bytes=VMEM_LIMIT)


def _rmsnorm(x, g):
    return x * lax.rsqrt(jnp.mean(x * x, axis=-1, keepdims=True) + NORM_EPS) * g


def _rope(u, c, sa, sb, half):
    return u * c + pltpu.roll(u, LANES - half, 1) * sa + pltpu.roll(u, half, 1) * sb


def _attn_in_kernel(x_ref, g_ref, w_ref, ch_ref, sha_ref, shb_ref, ci_ref, sia_ref, sib_ref,
                    ckw_ref, skwa_ref, skwb_ref,
                    q_ref, k_ref, v_ref, kb_ref, vb_ref, qi_ref, kw_ref, kib_ref):
    xn = _rmsnorm(x_ref[...], g_ref[...]).astype(BF16)
    z = jnp.dot(xn, w_ref[...], preferred_element_type=F32)
    ch, sha, shb = ch_ref[...], sha_ref[...], shb_ref[...]
    for h in range(N_HEADS):
        sl = slice(h * HEAD_DIM, (h + 1) * HEAD_DIM)
        q_ref[:, sl] = (_rope(z[:, sl], ch, sha, shb, ROT_DIM // 2) * ATTN_SCALE_LOG2).astype(BF16)
    for h in range(N_KV_HEADS):
        sl = slice(h * HEAD_DIM, (h + 1) * HEAD_DIM)
        kk = _rope(z[:, Q_W + h * HEAD_DIM:Q_W + (h + 1) * HEAD_DIM], ch, sha, shb, ROT_DIM // 2)
        k_ref[pl.ds(h, kk.shape[0], stride=N_KV_HEADS), :] = kk
        kb_ref[:, sl] = kk.astype(BF16)
    v = z[:, Q_W + KV_W:Q_W + 2 * KV_W]
    for h in range(N_KV_HEADS):
        v_ref[pl.ds(h, v.shape[0], stride=N_KV_HEADS), :] = v[:, h * HEAD_DIM:(h + 1) * HEAD_DIM]
    vb_ref[...] = v.astype(BF16)
    ci, sia, sib = ci_ref[...], sia_ref[...], sib_ref[...]
    qi0 = Q_W + 2 * KV_W
    for s in range(QI_W // LANES):
        sl = slice(s * LANES, (s + 1) * LANES)
        qi_ref[:, sl] = _rope(z[:, qi0 + s * LANES:qi0 + (s + 1) * LANES], ci, sia, sib,
                              IDX_ROT_DIM // 2).astype(BF16)
    kw = _rope(z[:, KW_COL:KW_COL + LANES], ckw_ref[...], skwa_ref[...], skwb_ref[...],
               IDX_ROT_DIM // 2)
    kw_ref[...] = kw
    kib_ref[...] = kw[:, :IDX_DIM].astype(BF16)


def _rope_tables(pos, rot, period, tail_scale=None):
    half = rot // 2
    freqs = ROPE_THETA ** (-(jnp.arange(half, dtype=F32) * 2.0) / rot)
    ang = pos.astype(F32)[:, None] * freqs[None, :]
    cos, sin = jnp.cos(ang), jnp.sin(ang)
    lane = np.arange(LANES) % period
    col = lane % half
    c = jnp.where(lane < rot, cos[:, col], 1.0)
    sa = jnp.where(lane < half, -sin[:, col], 0.0)
    sb = jnp.where((lane >= half) & (lane < rot), sin[:, col], 0.0)
    if tail_scale is not None:
        glane = np.arange(LANES)
        c = jnp.where(glane < period, c, 1.0)
        c = jnp.where((glane >= IDX_DIM) & (glane < IDX_DIM + N_IDX_HEADS), tail_scale, c)
        sa = jnp.where(glane < period, sa, 0.0)
        sb = jnp.where(glane < period, sb, 0.0)
    return c.astype(F32), sa.astype(F32), sb.astype(F32)


def _attn_in(x, pos, g, w_pad, tm):
    B, T, _ = x.shape
    ch, sha, shb = _rope_tables(pos, ROT_DIM, HEAD_DIM)
    ci, sia, sib = _rope_tables(pos, IDX_ROT_DIM, IDX_DIM)
    ckw, skwa, skwb = _rope_tables(pos, IDX_ROT_DIM, IDX_DIM, tail_scale=W_IDX_SCALE)
    row = lambda b, t: (b, t, 0)
    tab = lambda b, t: (t, 0)
    const = lambda b, t: (0, 0)
    blk = lambda w: pl.BlockSpec((None, tm, w), row)
    tspec = pl.BlockSpec((tm, LANES), tab)
    kvrows = pl.BlockSpec((None, tm * N_KV_HEADS, HEAD_DIM), row)
    outs = pl.pallas_call(
        _attn_in_kernel,
        grid=(B, T // tm),
        in_specs=[blk(D_MODEL), pl.BlockSpec((1, D_MODEL), const),
                  pl.BlockSpec((D_MODEL, ATTN_IN_PAD), const),
                  tspec, tspec, tspec, tspec, tspec, tspec, tspec, tspec, tspec],
        out_specs=[blk(Q_W), kvrows, kvrows, blk(KV_W), blk(KV_W), blk(QI_W),
                   blk(LANES), blk(IDX_DIM)],
        out_shape=[jax.ShapeDtypeStruct((B, T, Q_W), BF16),
                   jax.ShapeDtypeStruct((B, T * N_KV_HEADS, HEAD_DIM), F32),
                   jax.ShapeDtypeStruct((B, T * N_KV_HEADS, HEAD_DIM), F32),
                   jax.ShapeDtypeStruct((B, T, KV_W), BF16),
                   jax.ShapeDtypeStruct((B, T, KV_W), BF16),
                   jax.ShapeDtypeStruct((B, T, QI_W), BF16),
                   jax.ShapeDtypeStruct((B, T, LANES), F32),
                   jax.ShapeDtypeStruct((B, T, IDX_DIM), BF16)],
        compiler_params=_cparams(2),
        name="attn_in",
    )(x, g, w_pad, ch, sha, shb, ci, sia, sib, ckw, skwa, skwb)
    return outs


def _search_threshold(count_ge, bracket, scan, lo0, hi0, n0, kf):
    hi0 = jnp.where(n0 <= kf, lo0, hi0)

    def coarse(_, st):
        lo, hi = st
        p = 0.5 * lo + 0.5 * hi
        ge = count_ge(p) >= kf
        return jnp.where(ge, p, lo), jnp.where(ge, hi, p)

    lo, hi = lax.fori_loop(0, N_BISECT, coarse, (lo0, hi0))
    a, b = bracket(lo, hi)

    def n_open(a, b):
        return jnp.max(jnp.where(a < b, 1, 0).astype(jnp.int32))

    def body(carry):
        _, a, b = carry
        p = jnp.minimum(jnp.maximum(0.5 * a + 0.5 * b, a), b)
        p = jnp.where(p > a, p, b)
        cnt, lowmax, highmin = scan(p)
        act = a < b
        a2 = jnp.where(act, jnp.where(cnt >= kf, highmin, a), a)
        b2 = jnp.where(act, jnp.where(cnt >= kf, b, lowmax), b)
        return n_open(a2, b2), a2, b2

    out = lax.while_loop(lambda c: c[0] > 0, body, (n_open(a, b), a, b))
    return out[1]


ACC_ROWS = 64


def _rows_reduce8(x, op):
    return op(x.reshape(x.shape[0] // ACC_ROWS, ACC_ROWS, LANES), axis=0)


def _attn_prompt_kernel(q_ref, kb_ref, vb_ref, qi_ref, kib_ref, wi_ref, o_ref,
                        sc_ref, bias_ref, vext_ref, tri_ref, m_ref, acc_ref, *, n_sel):
    i = pl.program_id(1)
    nsup = i // (SUPER // KB) + 1
    kf = float(n_sel)
    inf = jnp.float32(jnp.inf)

    @pl.when(i == 0)
    def _():
        for g in range(N_KV_HEADS):
            vext_ref[g, :, :HEAD_DIM] = vb_ref[:, g * HEAD_DIM:(g + 1) * HEAD_DIM]
            vext_ref[g, :, HEAD_DIM:] = jnp.ones((vext_ref.shape[1], HEAD_DIM), BF16)
        r = lax.broadcasted_iota(jnp.int32, (RANK_BLOCK, RANK_BLOCK), 0)
        c = lax.broadcasted_iota(jnp.int32, (RANK_BLOCK, RANK_BLOCK), 1)
        tri_ref[...] = jnp.where(c <= r, 1.0, 0.0).astype(BF16)

    def rows(u):
        return pl.ds(pl.multiple_of(u * SUPER, SUPER), SUPER)

    def sweep(fn, init):
        return lax.fori_loop(0, nsup, lambda u, c: fn(sc_ref[rows(u), :], c), init)

    z8 = jnp.zeros((ACC_ROWS, LANES), F32)
    fold = lambda acc8, op: op(acc8, axis=0, keepdims=True)

    w = wi_ref[...]
    qi = qi_ref[...]
    qi_rows = jnp.concatenate([qi[:, h * IDX_DIM:(h + 1) * IDX_DIM] for h in range(N_IDX_HEADS)],
                              axis=0)
    kmq = (lax.broadcasted_iota(jnp.int32, (SUPER, KB), 0)
           - lax.broadcasted_iota(jnp.int32, (SUPER, KB), 1))

    def p1(u, carry):
        amin, amax = carry
        kic = kib_ref[rows(u), :]
        sh = lax.dot_general(kic, qi_rows, _NT, preferred_element_type=F32)
        s = None
        for h in range(N_IDX_HEADS):
            t = w[h:h + 1, :] * jnp.maximum(sh[:, h * KB:(h + 1) * KB], 0.0)
            s = t if s is None else s + t
        causal = kmq <= i * KB - u * SUPER
        x = jnp.where(causal, s, -inf)
        sc_ref[rows(u), :] = x
        amin = jnp.minimum(amin, _rows_reduce8(jnp.where(causal, s, inf), jnp.min))
        amax = jnp.maximum(amax, _rows_reduce8(x, jnp.max))
        return amin, amax

    amin8, amax8 = lax.fori_loop(0, nsup, p1, (z8 + inf, z8 - inf))
    lane = lax.broadcasted_iota(jnp.int32, (1, LANES), 1)
    n0 = (i * KB + lane + 1).astype(F32)

    def count_ge(p):
        return fold(sweep(lambda x, c: c + _rows_reduce8(jnp.where(x >= p, 1.0, 0.0), jnp.sum), z8),
                    jnp.sum)

    def bracket(lo, hi):
        def f(x, c):
            return (jnp.minimum(c[0], _rows_reduce8(jnp.where(x >= lo, x, inf), jnp.min)),
                    jnp.maximum(c[1], _rows_reduce8(jnp.where(x <= hi, x, -inf), jnp.max)))
        a8, b8 = sweep(f, (z8 + inf, z8 - inf))
        return fold(a8, jnp.min), fold(b8, jnp.max)

    def scan(p):
        def f(x, c):
            ge = x >= p
            return (c[0] + _rows_reduce8(jnp.where(ge, 1.0, 0.0), jnp.sum),
                    jnp.maximum(c[1], _rows_reduce8(jnp.where(ge, -inf, x), jnp.max)),
                    jnp.minimum(c[2], _rows_reduce8(jnp.where(ge, x, inf), jnp.min)))
        cnt, lo, hi = sweep(f, (z8, z8 - inf, z8 + inf))
        return fold(cnt, jnp.sum), fold(lo, jnp.max), fold(hi, jnp.min)

    thr = _search_threshold(count_ge, bracket, scan, fold(amin8, jnp.min), fold(amax8, jnp.max),
                            n0, kf)

    cgt = fold(sweep(lambda x, c: c + _rows_reduce8(jnp.where(x > thr, 1.0, 0.0), jnp.sum), z8),
               jnp.sum)
    need = kf - cgt

    tri = tri_ref[...]

    def p3(u, run):
        for part in range(SUPER // RANK_BLOCK):
            rs = pl.ds(pl.multiple_of(u * SUPER + part * RANK_BLOCK, RANK_BLOCK), RANK_BLOCK)
            x = sc_ref[rs, :]
            eq = x == thr
            rank = jnp.dot(tri, jnp.where(eq, 1.0, 0.0).astype(BF16),
                           preferred_element_type=F32) + run
            keep_tie = jnp.where(rank <= need, 0.0, MASK_BIAS)
            bias = jnp.where(x > thr, 0.0, jnp.where(eq, keep_tie, MASK_BIAS))
            bias_ref[rs, :] = bias.astype(BF16)
            run = rank[RANK_BLOCK - 1:RANK_BLOCK, :]
        return run

    lax.fori_loop(0, nsup, p3, jnp.zeros((1, LANES), F32))

    eye = jnp.where(lax.broadcasted_iota(jnp.int32, (KB, KB), 0)
                    == lax.broadcasted_iota(jnp.int32, (KB, KB), 1), 1.0, 0.0).astype(BF16)
    eye4 = jnp.concatenate([eye] * HEADS_PER_KV, axis=0)
    ags = []
    for g in range(N_KV_HEADS):
        qg = jnp.concatenate(
            [q_ref[:, (g * HEADS_PER_KV + r) * HEAD_DIM:(g * HEADS_PER_KV + r + 1) * HEAD_DIM]
             for r in range(HEADS_PER_KV)], axis=0)
        ags.append(jnp.concatenate([qg, eye4], axis=1))
    m_ref[...] = jnp.full(m_ref.shape, -inf, F32)
    acc_ref[...] = jnp.zeros(acc_ref.shape, F32)
    nrow = HEADS_PER_KV * KB
    part = nrow // ATTN_ROW_PARTS

    def qk(u):
        bias_t = bias_ref[rows(u), :]
        out = []
        for g in range(N_KV_HEADS):
            bmat = jnp.concatenate([kb_ref[rows(u), g * HEAD_DIM:(g + 1) * HEAD_DIM], bias_t],
                                   axis=1)
            out.append(lax.dot_general(ags[g], bmat, _NT,
                                       preferred_element_type=F32).astype(BF16))
        return tuple(out)

    def pa(u, ts):
        nxt = qk(jnp.minimum(u + 1, nsup - 1))
        for g in range(N_KV_HEADS):
            vx = vext_ref[g, rows(u), :]
            for hp in range(ATTN_ROW_PARTS):
                rs = slice(hp * part, (hp + 1) * part)
                t = ts[g][rs]
                m_old = m_ref[g, rs]
                m_new = jnp.maximum(m_old, jnp.max(t, axis=1, keepdims=True).astype(F32))
                p = jnp.exp2(t - m_new.astype(BF16))
                acc_ref[g, rs] = (jnp.exp2(m_old - m_new) * acc_ref[g, rs]
                                  + jnp.dot(p, vx, preferred_element_type=F32))
                m_ref[g, rs] = m_new
        return nxt

    lax.fori_loop(0, nsup, pa, qk(0))
    for g in range(N_KV_HEADS):
        acc = acc_ref[g]
        o = acc[:, :HEAD_DIM] / acc[:, HEAD_DIM:]
        for r in range(HEADS_PER_KV):
            h = g * HEADS_PER_KV + r
            o_ref[:, h * HEAD_DIM:(h + 1) * HEAD_DIM] = o[r * KB:(r + 1) * KB].astype(BF16)


def _attn_prompt(q, kb, vb, qi, kib, wi_t):
    B, S, _ = q.shape
    n_sel = min(TOPK_MAX, S // 4)
    qblk = lambda w: pl.BlockSpec((None, KB, w), lambda b, i: (b, i, 0))
    full = lambda w: pl.BlockSpec((None, S, w), lambda b, i: (b, 0, 0))
    nrow = HEADS_PER_KV * KB
    return pl.pallas_call(
        functools.partial(_attn_prompt_kernel, n_sel=n_sel),
        grid=(B, S // KB),
        in_specs=[qblk(Q_W), full(KV_W), full(KV_W), qblk(QI_W), full(IDX_DIM),
                  pl.BlockSpec((None, N_IDX_HEADS, KB), lambda b, i: (b, 0, i))],
        out_specs=qblk(Q_W),
        out_shape=jax.ShapeDtypeStruct((B, S, Q_W), BF16),
        scratch_shapes=[pltpu.VMEM((S, KB), F32), pltpu.VMEM((S, KB), BF16),
                        pltpu.VMEM((N_KV_HEADS, S, 2 * HEAD_DIM), BF16),
                        pltpu.VMEM((RANK_BLOCK, RANK_BLOCK), BF16),
                        pltpu.VMEM((N_KV_HEADS, nrow, 1), F32),
                        pltpu.VMEM((N_KV_HEADS, nrow, 2 * HEAD_DIM), F32)],
        compiler_params=_cparams(2),
        name="attn_prompt",
    )(q, kb, vb, qi, kib, wi_t)


def _idx_scores(qi, wcol, keys_t):
    s = jnp.dot(qi, keys_t, preferred_element_type=F32)
    r = jnp.maximum(s, 0.0) * wcol
    out = r[0:SUBLANES]
    for h in range(1, N_IDX_HEADS):
        out = out + r[h * SUBLANES:(h + 1) * SUBLANES]
    return out


def _score_sample_kernel(pt_ref, qi_ref, w_ref, *refs, past):
    ki_refs = refs[:PAGES_PER_STEP]
    kinew_ref, sc_ref = refs[PAGES_PER_STEP:]
    j = pl.program_id(1)
    qi = qi_ref[...]
    wcol = w_ref[:, 0:1]
    width = PAGES_PER_STEP * PAGE_SIZE
    keys_t = jnp.concatenate([r[...] for r in ki_refs], axis=1).astype(BF16)
    sc_ref[:, pl.ds(pl.multiple_of(j * width, width), width)] = _idx_scores(qi, wcol, keys_t)

    @pl.when(j == pl.num_programs(1) - 1)
    def _():
        inf = jnp.float32(jnp.inf)
        row = lax.broadcasted_iota(jnp.int32, (SUBLANES, PAGE_SIZE), 0)
        lane = lax.broadcasted_iota(jnp.int32, (SUBLANES, PAGE_SIZE), 1)
        snew = _idx_scores(qi, wcol, kinew_ref[...])
        sc_ref[:, past:past + PAGE_SIZE] = jnp.where(lane <= row, snew, -inf)
        sc_ref[:, past + PAGE_SIZE:] = jnp.full((SUBLANES, SUPER - PAGE_SIZE), -inf, F32)


def _score_sample(page_table, qi_pad, wcol, cache_ki_t, kinew_t):
    B, n_pages = page_table.shape
    past = n_pages * PAGE_SIZE
    steps = n_pages // PAGES_PER_STEP
    page = lambda p: pl.BlockSpec((None, IDX_DIM, PAGE_SIZE),
                                  lambda b, j, pt: (pt[b, j * PAGES_PER_STEP + p], 0, 0))
    per_b = lambda r, w: pl.BlockSpec((None, r, w), lambda b, j, pt: (b, 0, 0))
    return pl.pallas_call(
        functools.partial(_score_sample_kernel, past=past),
        grid_spec=pltpu.PrefetchScalarGridSpec(
            num_scalar_prefetch=1,
            grid=(B, steps),
            in_specs=[per_b(N_IDX_HEADS * SUBLANES, IDX_DIM), per_b(N_IDX_HEADS * SUBLANES, LANES)]
                     + [page(p) for p in range(PAGES_PER_STEP)]
                     + [per_b(IDX_DIM, PAGE_SIZE)],
            out_specs=per_b(SUBLANES, past + SUPER),
        ),
        out_shape=jax.ShapeDtypeStruct((B, SUBLANES, past + SUPER), F32),
        compiler_params=_cparams(2),
        name="score_sample",
    )(page_table, qi_pad, wcol, *([cache_ki_t] * PAGES_PER_STEP), kinew_t)


def _lanes_reduce(x, op):
    out = x[:, :LANES]
    for t in range(1, x.shape[1] // LANES):
        out = op(out, x[:, t * LANES:(t + 1) * LANES])
    return out


def _select_sample_kernel(sc_ref, sel_ref, tri_ref, spread_ref, *, n_sel):
    R, L = sc_ref.shape
    nsup = L // SUPER
    kf = float(n_sel)
    inf = jnp.float32(jnp.inf)
    r = lax.broadcasted_iota(jnp.int32, (SUPER, SUPER), 0)
    c = lax.broadcasted_iota(jnp.int32, (SUPER, SUPER), 1)
    tri_ref[...] = jnp.where(r <= c, 1.0, 0.0).astype(BF16)

    def cols(u):
        return pl.ds(pl.multiple_of(u * SUPER, SUPER), SUPER)

    def sweep(fn, init):
        return lax.fori_loop(0, nsup, lambda u, cr: fn(sc_ref[:, cols(u)], cr), init)

    zr = jnp.zeros((R, LANES), F32)
    rsum = lambda a: jnp.sum(a, axis=1, keepdims=True)
    rmax = lambda a: jnp.max(a, axis=1, keepdims=True)
    rmin = lambda a: jnp.min(a, axis=1, keepdims=True)
    one = lambda m: jnp.where(m, 1.0, 0.0)

    def f0(x, cr):
        fin = x > -inf
        return (jnp.minimum(cr[0], _lanes_reduce(jnp.where(fin, x, inf), jnp.minimum)),
                jnp.maximum(cr[1], _lanes_reduce(x, jnp.maximum)),
                cr[2] + _lanes_reduce(one(fin), jnp.add))
    lo8, hi8, n8 = sweep(f0, (zr + inf, zr - inf, zr))

    def count_ge(p):
        return rsum(sweep(lambda x, cr: cr + _lanes_reduce(one(x >= p), jnp.add), zr))

    def bracket(lo, hi):
        def f(x, cr):
            return (jnp.minimum(cr[0], _lanes_reduce(jnp.where(x >= lo, x, inf), jnp.minimum)),
                    jnp.maximum(cr[1], _lanes_reduce(jnp.where(x <= hi, x, -inf), jnp.maximum)))
        a8, b8 = sweep(f, (zr + inf, zr - inf))
        return rmin(a8), rmax(b8)

    def scan(p):
        def f(x, cr):
            ge = x >= p
            return (cr[0] + _lanes_reduce(one(ge), jnp.add),
                    jnp.maximum(cr[1], _lanes_reduce(jnp.where(ge, -inf, x), jnp.maximum)),
                    jnp.minimum(cr[2], _lanes_reduce(jnp.where(ge, x, inf), jnp.minimum)))
        cnt, lo, hi = sweep(f, (zr, zr - inf, zr + inf))
        return rsum(cnt), rmax(lo), rmin(hi)

    thr = _search_threshold(count_ge, bracket, scan, rmin(lo8), rmax(hi8), rsum(n8), kf)
    need = kf - rsum(sweep(lambda x, cr: cr + _lanes_reduce(one(x > thr), jnp.add), zr))

    wide = N_KV_HEADS * SUPER
    spread_ref[...] = jnp.where(
        lax.broadcasted_iota(jnp.int32, (SUPER, wide), 1) // N_KV_HEADS
        == lax.broadcasted_iota(jnp.int32, (SUPER, wide), 0), 1.0, 0.0).astype(BF16)

    def p3(u, run):
        x = sc_ref[:, cols(u)]
        eq = x == thr
        rank = jnp.dot(one(eq).astype(BF16), tri_ref[...], preferred_element_type=F32) + run
        sel = jnp.where(x > thr, 1.0, jnp.where(eq, one(rank <= need), 0.0))
        sel_ref[:, pl.ds(pl.multiple_of(u * wide, wide), wide)] = jnp.dot(
            sel.astype(BF16), spread_ref[...], preferred_element_type=F32)
        return rank[:, SUPER - 1:SUPER]

    lax.fori_loop(0, nsup, p3, jnp.zeros((R, 1), F32))


def _select_sample(sc, n_sel):
    R, L = sc.shape
    return pl.pallas_call(
        functools.partial(_select_sample_kernel, n_sel=n_sel),
        grid=(R // SEL_ROWS,),
        in_specs=[pl.BlockSpec((SEL_ROWS, L), lambda i: (i, 0))],
        out_specs=pl.BlockSpec((SEL_ROWS, N_KV_HEADS * L), lambda i: (i, 0)),
        out_shape=jax.ShapeDtypeStruct((R, N_KV_HEADS * L), F32),
        scratch_shapes=[pltpu.VMEM((SUPER, SUPER), BF16),
                        pltpu.VMEM((SUPER, N_KV_HEADS * SUPER), BF16)],
        compiler_params=_cparams(1),
        name="select_sample",
    )(sc)


def _attn_sample_kernel(pt_ref, q_ref, sel_ref, *refs, past):
    k_refs = refs[:PAGES_PER_STEP]
    v_refs = refs[PAGES_PER_STEP:2 * PAGES_PER_STEP]
    knew_ref, vnew_ref, o_ref, m_ref, acc_ref = refs[2 * PAGES_PER_STEP:]
    j = pl.program_id(1)
    nrow = N_HEADS * SUBLANES
    pw = N_KV_HEADS * PAGE_SIZE
    per_chain = PAGES_PER_STEP // SAMPLE_CHAINS

    @pl.when(j == 0)
    def _():
        m_ref[...] = jnp.full(m_ref.shape, -jnp.inf, F32)
        acc_ref[...] = jnp.zeros(acc_ref.shape, F32)

    q = q_ref[...]

    def update(ch, kk, vv, col):
        n = kk.shape[0]
        s = lax.dot_general(q, kk, _NT, preferred_element_type=F32)
        sel = jnp.concatenate([sel_ref[:, pl.ds(col, n)]] * N_HEADS, axis=0)
        own = (lax.broadcasted_iota(jnp.int32, (nrow, n), 0) // (HEADS_PER_KV * SUBLANES)
               == lax.broadcasted_iota(jnp.int32, (nrow, n), 1) % N_KV_HEADS)
        s = jnp.where(own, jnp.where(sel > 0.5, s, MASK_BIAS), MASK_BIAS)
        m_old = m_ref[ch]
        m_new = jnp.maximum(m_old, jnp.max(s, axis=1, keepdims=True))
        pb = jnp.exp2(s - m_new).astype(BF16)
        vext = jnp.concatenate([vv, jnp.ones((n, HEAD_DIM), BF16)], axis=1)
        acc_ref[ch] = (jnp.exp2(m_old - m_new) * acc_ref[ch]
                       + jnp.dot(pb, vext, preferred_element_type=F32))
        m_ref[ch] = m_new

    for ch in range(SAMPLE_CHAINS):
        pages = range(ch * per_chain, (ch + 1) * per_chain)
        kk = jnp.concatenate([k_refs[p][...].astype(BF16) for p in pages], axis=0)
        vv = jnp.concatenate([v_refs[p][...].astype(BF16) for p in pages], axis=0)
        col = pl.multiple_of((j * PAGES_PER_STEP + ch * per_chain) * pw, per_chain * pw)
        update(ch, kk, vv, col)

    @pl.when(j == pl.num_programs(1) - 1)
    def _():
        update(0, knew_ref[...], vnew_ref[...], N_KV_HEADS * past)
        m = m_ref[0]
        for ch in range(1, SAMPLE_CHAINS):
            m = jnp.maximum(m, m_ref[ch])
        acc = jnp.exp2(m_ref[0] - m) * acc_ref[0]
        for ch in range(1, SAMPLE_CHAINS):
            acc = acc + jnp.exp2(m_ref[ch] - m) * acc_ref[ch]
        o_ref[...] = acc[:, :HEAD_DIM] / acc[:, HEAD_DIM:]


def _attn_sample(page_table, q_all, sel, cache_k, cache_v, knew, vnew):
    B, n_pages = page_table.shape
    past = n_pages * PAGE_SIZE
    steps = n_pages // PAGES_PER_STEP
    nrow = N_HEADS * SUBLANES
    pw = N_KV_HEADS * PAGE_SIZE
    page = lambda p: pl.BlockSpec((pw, HEAD_DIM), lambda b, j, pt: (pt[b, j * PAGES_PER_STEP + p], 0))
    per_b = lambda r, w: pl.BlockSpec((None, r, w), lambda b, j, pt: (b, 0, 0))
    return pl.pallas_call(
        functools.partial(_attn_sample_kernel, past=past),
        grid_spec=pltpu.PrefetchScalarGridSpec(
            num_scalar_prefetch=1,
            grid=(B, steps),
            in_specs=[per_b(nrow, HEAD_DIM), per_b(SUBLANES, sel.shape[2])]
                     + [page(p) for p in range(PAGES_PER_STEP)] * 2
                     + [per_b(pw, HEAD_DIM), per_b(pw, HEAD_DIM)],
            out_specs=per_b(nrow, HEAD_DIM),
            scratch_shapes=[pltpu.VMEM((SAMPLE_CHAINS, nrow, 1), F32),
                            pltpu.VMEM((SAMPLE_CHAINS, nrow, 2 * HEAD_DIM), F32)],
        ),
        out_shape=jax.ShapeDtypeStruct((B, nrow, HEAD_DIM), F32),
        compiler_params=_cparams(2),
        name="attn_sample",
    )(page_table, q_all, sel, *([cache_k] * PAGES_PER_STEP), *([cache_v] * PAGES_PER_STEP), knew, vnew)


def _ffn_kernel(*refs, sample, final):
    it = iter(refs)
    h_ref, o_ref, wout_ref, g_ref, wup_ref, cw_ref, cb_ref, wdown_ref = (next(it) for _ in range(8))
    gfin_ref = next(it) if final else None
    st1_ref, st2_ref = (next(it), next(it)) if sample else (None, None)
    hout_ref, cst_ref = next(it), next(it)
    act_ref = next(it)
    carry_ref = None if sample else next(it)

    tm = h_ref.shape[0]
    h1 = h_ref[...] + jnp.dot(o_ref[...], wout_ref[...], preferred_element_type=F32)
    xn = _rmsnorm(h1, g_ref[...]).astype(BF16)

    if sample:
        t4 = lax.broadcasted_iota(jnp.int32, (tm, FF_CHUNK), 0) % 4
    else:
        @pl.when(pl.program_id(1) == 0)
        def _():
            carry_ref[...] = jnp.zeros(carry_ref.shape, F32)
        row8 = lax.broadcasted_iota(jnp.int32, (SUBLANES, FF_CHUNK), 0)

    def conv(cols):
        u = jnp.dot(xn, wup_ref[:, cols], preferred_element_type=F32)
        r1 = pltpu.roll(u, 1, 0)
        r2 = pltpu.roll(u, 2, 0)
        if sample:
            s1 = jnp.where(t4 == 0, st1_ref[:, cols], r1)
            s2 = jnp.where(t4 == 0, st2_ref[:, cols], jnp.where(t4 == 1, st1_ref[:, cols], r2))
            cst_ref[:, cols] = u
        else:
            prev = carry_ref[:, cols]
            top1 = jnp.where(row8 < 1, pltpu.roll(prev, 1, 0), r1[:SUBLANES])
            top2 = jnp.where(row8 < 2, pltpu.roll(prev, 2, 0), r2[:SUBLANES])
            s1 = jnp.concatenate([top1, r1[SUBLANES:]], axis=0)
            s2 = jnp.concatenate([top2, r2[SUBLANES:]], axis=0)
            carry_ref[:, cols] = u[tm - SUBLANES:]
            cst_ref[:, cols] = u[tm - SUBLANES:]
        cw = cw_ref[:, cols]
        return cb_ref[:, cols] + cw[0:1] * s2 + cw[1:2] * s1 + cw[2:3] * u

    for c in range(N_FF_CHUNKS):
        gate = conv(slice(c * FF_CHUNK, (c + 1) * FF_CHUNK))
        val = conv(slice(D_FF + c * FF_CHUNK, D_FF + (c + 1) * FF_CHUNK))
        act_ref[:, c * FF_CHUNK:(c + 1) * FF_CHUNK] = (
            gate * (1.0 / (1.0 + jnp.exp(-gate))) * val).astype(BF16)

    out = h1 + jnp.dot(act_ref[...], wdown_ref[...], preferred_element_type=F32)
    if final:
        out = _rmsnorm(out, gfin_ref[...])
    hout_ref[...] = out


def _ffn(h, o, wout, g, wup, cw, cb, wdown, gfin, state, tm):
    B, T, _ = h.shape
    sample = state is not None
    final = gfin is not None
    row = lambda b, t: (b, t, 0)
    const = lambda b, t: (0, 0)
    w = lambda a: pl.BlockSpec(a.shape, const, pipeline_mode=pl.Buffered(1))
    operands = [h, o, wout, g, wup, cw, cb, wdown]
    in_specs = [pl.BlockSpec((None, tm, D_MODEL), row), pl.BlockSpec((None, tm, D_MODEL), row),
                w(wout), w(g), w(wup), w(cw), w(cb), w(wdown)]
    if final:
        operands.append(gfin)
        in_specs.append(w(gfin))
    scratch = [pltpu.VMEM((tm, D_FF), BF16)]
    if sample:
        operands += [state[0], state[1]]
        in_specs += [w(state[0]), w(state[1])]
        cst_shape = jax.ShapeDtypeStruct((B, T, 2 * D_FF), F32)
        cst_spec = pl.BlockSpec((None, tm, 2 * D_FF), row)
    else:
        cst_shape = jax.ShapeDtypeStruct((B, SUBLANES, 2 * D_FF), F32)
        cst_spec = pl.BlockSpec((None, SUBLANES, 2 * D_FF), lambda b, t: (b, 0, 0))
        scratch.append(pltpu.VMEM((SUBLANES, 2 * D_FF), F32))
    return pl.pallas_call(
        functools.partial(_ffn_kernel, sample=sample, final=final),
        grid=(B, T // tm),
        in_specs=in_specs,
        out_specs=[pl.BlockSpec((None, tm, D_MODEL), row), cst_spec],
        out_shape=[jax.ShapeDtypeStruct((B, T, D_MODEL), F32), cst_shape],
        scratch_shapes=scratch,
        compiler_params=_cparams(2),
        name="ffn_sample" if sample else "ffn_prompt",
    )(*operands)


def _sgu_kernel(h_ref, g_ref, win_ref, lng_ref, lnb_ref, ws_ref, bias_ref, *out_refs, emit_v):
    y_ref = out_refs[0]
    tm = h_ref.shape[0]
    xn = _rmsnorm(h_ref[...], g_ref[...]).astype(BF16)
    uv = jnp.dot(xn, win_ref[...], preferred_element_type=F32)
    u = uv[:, :D_MODEL]
    v = uv[:, D_MODEL:]
    xc = v - jnp.mean(v, axis=-1, keepdims=True)
    vn = xc * lax.rsqrt(jnp.mean(xc * xc, axis=-1, keepdims=True) + LN_EPS) * lng_ref[...] + lnb_ref[...]
    if emit_v:
        out_refs[1][...] = vn
    vnb = vn.astype(BF16)
    tri = (lax.broadcasted_iota(jnp.int32, (CHUNK, CHUNK), 0)
           >= lax.broadcasted_iota(jnp.int32, (CHUNK, CHUNK), 1))
    for g in range(SGU_GROUPS):
        wg = jnp.where(tri, ws_ref[g], 0.0).astype(BF16)
        cols = slice(g * SGU_GROUP_DIM, (g + 1) * SGU_GROUP_DIM)
        for c in range(tm // CHUNK):
            rows = slice(c * CHUNK, (c + 1) * CHUNK)
            s = jnp.dot(wg, vnb[rows, cols], preferred_element_type=F32) + bias_ref[:, cols]
            y_ref[rows, cols] = (u[rows, cols] * s).astype(BF16)


def _sgu_bias(bs):
    return jnp.repeat(bs.T, SGU_GROUP_DIM, axis=1)


def _sgu_sample_weights(ws, bs, B, T):
    corner = ws[:, :T, :T]
    ws_s = jnp.einsum('bc,gts->gbtcs', jnp.eye(B, dtype=ws.dtype), corner)
    return ws_s.reshape(SGU_GROUPS, B * T, B * T), jnp.tile(_sgu_bias(bs)[:T], (B, 1))


def _sgu(h, g, win, lng, lnb, ws, bias, tm, emit_v):
    B, T, _ = h.shape
    row = lambda b, t: (b, t, 0)
    w = lambda a: pl.BlockSpec(a.shape, lambda b, t: (0,) * a.ndim)
    out_shape = [jax.ShapeDtypeStruct((B, T, D_MODEL), BF16)]
    out_specs = [pl.BlockSpec((None, tm, D_MODEL), row)]
    if emit_v:
        out_shape.append(jax.ShapeDtypeStruct((B, T, D_MODEL), F32))
        out_specs.append(pl.BlockSpec((None, tm, D_MODEL), row))
    return pl.pallas_call(
        functools.partial(_sgu_kernel, emit_v=emit_v),
        grid=(B, T // tm),
        in_specs=[pl.BlockSpec((None, tm, D_MODEL), row), w(g), w(win), w(lng), w(lnb), w(ws), w(bias)],
        out_specs=out_specs,
        out_shape=out_shape,
        compiler_params=_cparams(2),
        name="sgu",
    )(h, g, win, lng, lnb, ws, bias)


def _pad_axis(a, axis, size):
    pad = [(0, 0)] * a.ndim
    pad[axis] = (0, size - a.shape[axis])
    return jnp.pad(a, pad)


def _sample_attention(qs, ksb, vsb, qis, kws, kisb, page_table, cache_k, cache_v, cache_ki, Bs, Ts):
    Ns = Bs * Ts
    n_pool = cache_k.shape[1]
    past = page_table.shape[1] * PAGE_SIZE
    qi_pad = _pad_axis(qis.reshape(Bs, Ts, N_IDX_HEADS, IDX_DIM).transpose(0, 2, 1, 3), 2, SUBLANES)
    qi_pad = qi_pad.reshape(Bs, N_IDX_HEADS * SUBLANES, IDX_DIM)
    wcol = kws[0, :, IDX_DIM:IDX_DIM + N_IDX_HEADS].reshape(Bs, Ts, N_IDX_HEADS).transpose(0, 2, 1)
    wcol = _pad_axis(wcol, 2, SUBLANES).reshape(Bs, N_IDX_HEADS * SUBLANES, 1)
    wcol = jnp.broadcast_to(wcol, (Bs, N_IDX_HEADS * SUBLANES, LANES))
    kinew_t = jnp.swapaxes(_pad_axis(kisb.reshape(Bs, Ts, IDX_DIM), 1, PAGE_SIZE), 1, 2)
    cache_ki_t = jnp.swapaxes(cache_ki, 2, 3).reshape(n_pool, IDX_DIM, PAGE_SIZE)
    sc = _score_sample(page_table, qi_pad, wcol, cache_ki_t, kinew_t)
    n_sel = min(TOPK_MAX, (past + Ts) // 4)
    sel = _select_sample(sc.reshape(Bs * SUBLANES, past + SUPER), n_sel)
    sel = sel.reshape(Bs, SUBLANES, N_KV_HEADS * (past + SUPER))
    q_all = qs.reshape(Bs, Ts, N_HEADS, HEAD_DIM).transpose(0, 2, 1, 3)
    q_all = _pad_axis(q_all, 2, SUBLANES).reshape(Bs, N_HEADS * SUBLANES, HEAD_DIM)
    pw = N_KV_HEADS * PAGE_SIZE
    knew = _pad_axis(ksb.reshape(Bs, Ts * N_KV_HEADS, HEAD_DIM), 1, pw)
    vnew = _pad_axis(vsb.reshape(Bs, Ts * N_KV_HEADS, HEAD_DIM), 1, pw)
    o_s = _attn_sample(page_table, q_all, sel, cache_k.reshape(n_pool * pw, HEAD_DIM),
                       cache_v.reshape(n_pool * pw, HEAD_DIM), knew, vnew)
    o_s = o_s.reshape(Bs, N_HEADS, SUBLANES, HEAD_DIM)[:, :, :Ts]
    return o_s.transpose(0, 2, 1, 3).reshape(1, Ns, Q_W).astype(BF16)


def kernel(x_prompt, x_sample, cache_k, cache_v, cache_idx_k, state_ffn_conv, page_table, attn_norm,
           w_attn_in, w_attn_out, sgu_norm, w_sgu_in, sgu_ln_g, sgu_ln_b, sgu_ws, sgu_bs, w_sgu_out,
           ffn_norm, w_ffn_up, ffn_conv_w, ffn_conv_b, w_ffn_down, final_norm):
    B, S, _ = x_prompt.shape
    Bs, Ts, _ = x_sample.shape
    Ns = Bs * Ts
    n_pages = page_table.shape[1]
    past = n_pages * PAGE_SIZE
    assert w_ffn_up.shape[0] == 2 and w_attn_in.shape[0] == 1 and w_sgu_in.shape[0] == 1
    assert Ns == CHUNK and Ts == 4 and S % SUPER == 0 and n_pages % PAGES_PER_STEP == 0
    bf = lambda a: a.astype(BF16)
    r1 = lambda a: a[None]
    tm_p, tm_s = 512, Ns
    tm_ffn = FFN_ROWS if S % FFN_ROWS == 0 else tm_p

    w_in = bf(_pad_axis(w_attn_in[0], 1, ATTN_IN_PAD))
    g_attn = r1(attn_norm[0])
    q, k32, v32, kb, vb, qi, kw, kib = _attn_in(x_prompt, jnp.arange(S), g_attn, w_in, tm_p)
    wi_t = jnp.swapaxes(kw[..., IDX_DIM:IDX_DIM + N_IDX_HEADS], 1, 2)
    o_p = _attn_prompt(q, kb, vb, qi, kib, wi_t)

    xs = x_sample.reshape(1, Ns, D_MODEL)
    pos_s = past + (jnp.arange(Ns) % Ts)
    qs, ks32, vs32, ksb, vsb, qis, kws, kisb = _attn_in(xs, pos_s, g_attn, w_in, tm_s)
    o_s = _sample_attention(qs, ksb, vsb, qis, kws, kisb, page_table,
                            cache_k[0:1], cache_v[0:1], cache_idx_k[0:1], Bs, Ts)

    def ffn(i, h, o, wout, gfin, sample):
        state = None
        if sample:
            st = state_ffn_conv[i]
            state = (jnp.repeat(st[:, 1], Ts, axis=0), jnp.repeat(st[:, 0], Ts, axis=0))
        return _ffn(h, o, bf(wout), r1(ffn_norm[i]), bf(w_ffn_up[i]), ffn_conv_w[i],
                    r1(ffn_conv_b[i]), bf(w_ffn_down[i]), gfin, state, tm_s if sample else tm_ffn)

    hp, cst_p0 = ffn(0, x_prompt, o_p, w_attn_out[0], None, False)
    hs, up_s0 = ffn(0, xs, o_s, w_attn_out[0], None, True)

    sgu_args = (r1(sgu_norm[0]), bf(w_sgu_in[0]), r1(sgu_ln_g[0]), r1(sgu_ln_b[0]))
    (y_p,) = _sgu(hp, *sgu_args, sgu_ws[0], _sgu_bias(sgu_bs[0]), tm_p, False)
    ws_s, bias_s = _sgu_sample_weights(sgu_ws[0], sgu_bs[0], Bs, Ts)
    y_s, vn_s = _sgu(hs, *sgu_args, ws_s, bias_s, tm_s, True)

    gfin = r1(final_norm)
    yp, cst_p1 = ffn(1, hp, y_p, w_sgu_out[0], gfin, False)
    ys, up_s1 = ffn(1, hs, y_s, w_sgu_out[0], gfin, True)

    tail = lambda c: c[:, SUBLANES - (CONV_W - 1):]
    stail = lambda u: u.reshape(Bs, Ts, 2 * D_FF)[:, Ts - (CONV_W - 1):]
    return (yp, ys.reshape(Bs, Ts, D_MODEL),
            k32.reshape(1, B, S, N_KV_HEADS, HEAD_DIM), v32.reshape(1, B, S, N_KV_HEADS, HEAD_DIM),
            kw[None, ..., :IDX_DIM],
            ks32.reshape(1, Bs, Ts, N_KV_HEADS, HEAD_DIM), vs32.reshape(1, Bs, Ts, N_KV_HEADS, HEAD_DIM),
            kws[..., :IDX_DIM].reshape(1, Bs, Ts, IDX_DIM),
            vn_s.reshape(1, Bs, Ts, D_MODEL),
            jnp.stack([tail(cst_p0), tail(cst_p1)]),
            jnp.stack([stail(up_s0), stail(up_s1)]))
```

```python
import functools

import numpy as np
import jax
import jax.numpy as jnp
from jax import lax
from jax.experimental import pallas as pl
from jax.experimental.pallas import tpu as pltpu

F32 = jnp.float32
BF16 = jnp.bfloat16

D_MODEL = 1024
N_HEADS = 8
HEAD_DIM = 128
N_KV_HEADS = 2
HEADS_PER_KV = N_HEADS // N_KV_HEADS
ROT_DIM = 32
ROPE_THETA = 500000.0
N_IDX_HEADS = 4
IDX_DIM = 64
IDX_ROT_DIM = 16
TOPK_MAX = 256
PAGE_SIZE = 128
Q_W = N_HEADS * HEAD_DIM
KV_W = N_KV_HEADS * HEAD_DIM
QI_W = N_IDX_HEADS * IDX_DIM
ATTN_IN = Q_W + 2 * KV_W + QI_W + IDX_DIM + N_IDX_HEADS
ATTN_IN_PAD = 1920
KW_COL = Q_W + 2 * KV_W + QI_W
CHUNK = 128
SGU_GROUPS = 4
SGU_GROUP_DIM = D_MODEL // SGU_GROUPS
D_FF = 2816
CONV_W = 3
NORM_EPS = 1e-6
LN_EPS = 1e-5
LOG2E = 1.4426950408889634
ATTN_SCALE_LOG2 = HEAD_DIM ** -0.5 * LOG2E
W_IDX_SCALE = (N_IDX_HEADS ** -0.5) * (IDX_DIM ** -0.5)

LANES = 128
SUBLANES = 8
KB = 128
SUPER = 512
ATTN_ROW_PARTS = 2
RANK_BLOCK = 256
FF_CHUNK = 256
FFN_ROWS = 1024
N_FF_CHUNKS = D_FF // FF_CHUNK
PAGES_PER_STEP = 16
IDX_PAGES_PER_STEP = 32
SAMPLE_CHAINS = 2
SEL_ROWS = 64
N_BISECT = 20
VMEM_LIMIT = 56 * 1024 * 1024
MASK_BIAS = -1e30

_NT = (((1,), (1,)), ((), ()))


def _cparams(n_axes):
    return pltpu.CompilerParams(
        dimension_semantics=("arbitrary",) * n_axes, vmem_limit_bytes=VMEM_LIMIT)


def _rmsnorm(x, g):
    return x * lax.rsqrt(jnp.mean(x * x, axis=-1, keepdims=True) + NORM_EPS) * g


def _rope(u, c, sa, sb, half):
    return u * c + pltpu.roll(u, LANES - half, 1) * sa + pltpu.roll(u, half, 1) * sb


def _attn_in_kernel(x_ref, g_ref, w_ref, ch_ref, sha_ref, shb_ref, ci_ref, sia_ref, sib_ref,
                    ckw_ref, skwa_ref, skwb_ref,
                    q_ref, k_ref, v_ref, kb_ref, vb_ref, qi_ref, kw_ref, kib_ref):
    xn = _rmsnorm(x_ref[...], g_ref[...]).astype(BF16)
    z = jnp.dot(xn, w_ref[...], preferred_element_type=F32)
    ch, sha, shb = ch_ref[...], sha_ref[...], shb_ref[...]
    for h in range(N_HEADS):
        sl = slice(h * HEAD_DIM, (h + 1) * HEAD_DIM)
        q_ref[:, sl] = (_rope(z[:, sl], ch, sha, shb, ROT_DIM // 2) * ATTN_SCALE_LOG2).astype(BF16)
    for h in range(N_KV_HEADS):
        sl = slice(h * HEAD_DIM, (h + 1) * HEAD_DIM)
        kk = _rope(z[:, Q_W + h * HEAD_DIM:Q_W + (h + 1) * HEAD_DIM], ch, sha, shb, ROT_DIM // 2)
        k_ref[pl.ds(h, kk.shape[0], stride=N_KV_HEADS), :] = kk
        kb_ref[:, sl] = kk.astype(BF16)
    v = z[:, Q_W + KV_W:Q_W + 2 * KV_W]
    for h in range(N_KV_HEADS):
        v_ref[pl.ds(h, v.shape[0], stride=N_KV_HEADS), :] = v[:, h * HEAD_DIM:(h + 1) * HEAD_DIM]
    vb_ref[...] = v.astype(BF16)
    ci, sia, sib = ci_ref[...], sia_ref[...], sib_ref[...]
    qi0 = Q_W + 2 * KV_W
    for s in range(QI_W // LANES):
        sl = slice(s * LANES, (s + 1) * LANES)
        qi_ref[:, sl] = _rope(z[:, qi0 + s * LANES:qi0 + (s + 1) * LANES], ci, sia, sib,
                              IDX_ROT_DIM // 2).astype(BF16)
    kw = _rope(z[:, KW_COL:KW_COL + LANES], ckw_ref[...], skwa_ref[...], skwb_ref[...],
               IDX_ROT_DIM // 2)
    kw_ref[...] = kw
    kib_ref[...] = kw[:, :IDX_DIM].astype(BF16)


def _rope_tables(pos, rot, period, tail_scale=None):
    half = rot // 2
    freqs = ROPE_THETA ** (-(jnp.arange(half, dtype=F32) * 2.0) / rot)
    ang = pos.astype(F32)[:, None] * freqs[None, :]
    cos, sin = jnp.cos(ang), jnp.sin(ang)
    lane = np.arange(LANES) % period
    col = lane % half
    c = jnp.where(lane < rot, cos[:, col], 1.0)
    sa = jnp.where(lane < half, -sin[:, col], 0.0)
    sb = jnp.where((lane >= half) & (lane < rot), sin[:, col], 0.0)
    if tail_scale is not None:
        glane = np.arange(LANES)
        c = jnp.where(glane < period, c, 1.0)
        c = jnp.where((glane >= IDX_DIM) & (glane < IDX_DIM + N_IDX_HEADS), tail_scale, c)
        sa = jnp.where(glane < period, sa, 0.0)
        sb = jnp.where(glane < period, sb, 0.0)
    return c.astype(F32), sa.astype(F32), sb.astype(F32)


def _attn_in(x, pos, g, w_pad, tm):
    B, T, _ = x.shape
    ch, sha, shb = _rope_tables(pos, ROT_DIM, HEAD_DIM)
    ci, sia, sib = _rope_tables(pos, IDX_ROT_DIM, IDX_DIM)
    ckw, skwa, skwb = _rope_tables(pos, IDX_ROT_DIM, IDX_DIM, tail_scale=W_IDX_SCALE)
    row = lambda b, t: (b, t, 0)
    tab = lambda b, t: (t, 0)
    const = lambda b, t: (0, 0)
    blk = lambda w: pl.BlockSpec((None, tm, w), row)
    tspec = pl.BlockSpec((tm, LANES), tab)
    kvrows = pl.BlockSpec((None, tm * N_KV_HEADS, HEAD_DIM), row)
    outs = pl.pallas_call(
        _attn_in_kernel,
        grid=(B, T // tm),
        in_specs=[blk(D_MODEL), pl.BlockSpec((1, D_MODEL), const),
                  pl.BlockSpec((D_MODEL, ATTN_IN_PAD), const),
                  tspec, tspec, tspec, tspec, tspec, tspec, tspec, tspec, tspec],
        out_specs=[blk(Q_W), kvrows, kvrows, blk(KV_W), blk(KV_W), blk(QI_W),
                   blk(LANES), blk(IDX_DIM)],
        out_shape=[jax.ShapeDtypeStruct((B, T, Q_W), BF16),
                   jax.ShapeDtypeStruct((B, T * N_KV_HEADS, HEAD_DIM), F32),
                   jax.ShapeDtypeStruct((B, T * N_KV_HEADS, HEAD_DIM), F32),
                   jax.ShapeDtypeStruct((B, T, KV_W), BF16),
                   jax.ShapeDtypeStruct((B, T, KV_W), BF16),
                   jax.ShapeDtypeStruct((B, T, QI_W), BF16),
                   jax.ShapeDtypeStruct((B, T, LANES), F32),
                   jax.ShapeDtypeStruct((B, T, IDX_DIM), BF16)],
        compiler_params=_cparams(2),
        name="attn_in",
    )(x, g, w_pad, ch, sha, shb, ci, sia, sib, ckw, skwa, skwb)
    return outs


def _search_threshold(count_ge, bracket, scan, lo0, hi0, n0, kf):
    hi0 = jnp.where(n0 <= kf, lo0, hi0)

    def coarse(_, st):
        lo, hi = st
        p = 0.5 * lo + 0.5 * hi
        ge = count_ge(p) >= kf
        return jnp.where(ge, p, lo), jnp.where(ge, hi, p)

    lo, hi = lax.fori_loop(0, N_BISECT, coarse, (lo0, hi0))
    a, b = bracket(lo, hi)

    def n_open(a, b):
        return jnp.max(jnp.where(a < b, 1, 0).astype(jnp.int32))

    def body(carry):
        _, a, b = carry
        p = jnp.minimum(jnp.maximum(0.5 * a + 0.5 * b, a), b)
        p = jnp.where(p > a, p, b)
        cnt, lowmax, highmin = scan(p)
        act = a < b
        a2 = jnp.where(act, jnp.where(cnt >= kf, highmin, a), a)
        b2 = jnp.where(act, jnp.where(cnt >= kf, b, lowmax), b)
        return n_open(a2, b2), a2, b2

    out = lax.while_loop(lambda c: c[0] > 0, body, (n_open(a, b), a, b))
    return out[1]


ACC_ROWS = 64


def _rows_reduce8(x, op):
    return op(x.reshape(x.shape[0] // ACC_ROWS, ACC_ROWS, LANES), axis=0)


def _attn_prompt_kernel(q_ref, kb_ref, vb_ref, qi_ref, kib_ref, wi_ref, o_ref,
                        sc_ref, bias_ref, vext_ref, tri_ref, m_ref, acc_ref, *, n_sel):
    i = pl.program_id(1)
    nsup = i // (SUPER // KB) + 1
    kf = float(n_sel)
    inf = jnp.float32(jnp.inf)

    @pl.when(i == 0)
    def _():
        for g in range(N_KV_HEADS):
            vext_ref[g, :, :HEAD_DIM] = vb_ref[:, g * HEAD_DIM:(g + 1) * HEAD_DIM]
            vext_ref[g, :, HEAD_DIM:] = jnp.ones((vext_ref.shape[1], HEAD_DIM), BF16)
        r = lax.broadcasted_iota(jnp.int32, (RANK_BLOCK, RANK_BLOCK), 0)
        c = lax.broadcasted_iota(jnp.int32, (RANK_BLOCK, RANK_BLOCK), 1)
        tri_ref[...] = jnp.where(c <= r, 1.0, 0.0).astype(BF16)

    def rows(u):
        return pl.ds(pl.multiple_of(u * SUPER, SUPER), SUPER)

    def sweep(fn, init):
        return lax.fori_loop(0, nsup, lambda u, c: fn(sc_ref[rows(u), :], c), init)

    z8 = jnp.zeros((ACC_ROWS, LANES), F32)
    fold = lambda acc8, op: op(acc8, axis=0, keepdims=True)

    w = wi_ref[...]
    qi = qi_ref[...]
    qi_rows = jnp.concatenate([qi[:, h * IDX_DIM:(h + 1) * IDX_DIM] for h in range(N_IDX_HEADS)],
                              axis=0)
    kmq = (lax.broadcasted_iota(jnp.int32, (SUPER, KB), 0)
           - lax.broadcasted_iota(jnp.int32, (SUPER, KB), 1))

    def p1(u, carry):
        amin, amax = carry
        kic = kib_ref[rows(u), :]
        sh = lax.dot_general(kic, qi_rows, _NT, preferred_element_type=F32)
        s = None
        for h in range(N_IDX_HEADS):
            t = w[h:h + 1, :] * jnp.maximum(sh[:, h * KB:(h + 1) * KB], 0.0)
            s = t if s is None else s + t
        causal = kmq <= i * KB - u * SUPER
        x = jnp.where(causal, s, -inf)
        sc_ref[rows(u), :] = x
        amin = jnp.minimum(amin, _rows_reduce8(jnp.where(causal, s, inf), jnp.min))
        amax = jnp.maximum(amax, _rows_reduce8(x, jnp.max))
        return amin, amax

    amin8, amax8 = lax.fori_loop(0, nsup, p1, (z8 + inf, z8 - inf))
    lane = lax.broadcasted_iota(jnp.int32, (1, LANES), 1)
    n0 = (i * KB + lane + 1).astype(F32)

    def count_ge(p):
        return fold(sweep(lambda x, c: c + _rows_reduce8(jnp.where(x >= p, 1.0, 0.0), jnp.sum), z8),
                    jnp.sum)

    def bracket(lo, hi):
        def f(x, c):
            return (jnp.minimum(c[0], _rows_reduce8(jnp.where(x >= lo, x, inf), jnp.min)),
                    jnp.maximum(c[1], _rows_reduce8(jnp.where(x <= hi, x, -inf), jnp.max)))
        a8, b8 = sweep(f, (z8 + inf, z8 - inf))
        return fold(a8, jnp.min), fold(b8, jnp.max)

    def scan(p):
        def f(x, c):
            ge = x >= p
            return (c[0] + _rows_reduce8(jnp.where(ge, 1.0, 0.0), jnp.sum),
                    jnp.maximum(c[1], _rows_reduce8(jnp.where(ge, -inf, x), jnp.max)),
                    jnp.minimum(c[2], _rows_reduce8(jnp.where(ge, x, inf), jnp.min)))
        cnt, lo, hi = sweep(f, (z8, z8 - inf, z8 + inf))
        return fold(cnt, jnp.sum), fold(lo, jnp.max), fold(hi, jnp.min)

    thr = _search_threshold(count_ge, bracket, scan, fold(amin8, jnp.min), fold(amax8, jnp.max),
                            n0, kf)

    cgt = fold(sweep(lambda x, c: c + _rows_reduce8(jnp.where(x > thr, 1.0, 0.0), jnp.sum), z8),
               jnp.sum)
    need = kf - cgt

    tri = tri_ref[...]

    def p3(u, run):
        for part in range(SUPER // RANK_BLOCK):
            rs = pl.ds(pl.multiple_of(u * SUPER + part * RANK_BLOCK, RANK_BLOCK), RANK_BLOCK)
            x = sc_ref[rs, :]
            eq = x == thr
            rank = jnp.dot(tri, jnp.where(eq, 1.0, 0.0).astype(BF16),
                           preferred_element_type=F32) + run
            keep_tie = jnp.where(rank <= need, 0.0, MASK_BIAS)
            bias = jnp.where(x > thr, 0.0, jnp.where(eq, keep_tie, MASK_BIAS))
            bias_ref[rs, :] = bias.astype(BF16)
            run = rank[RANK_BLOCK - 1:RANK_BLOCK, :]
        return run

    lax.fori_loop(0, nsup, p3, jnp.zeros((1, LANES), F32))

    eye = jnp.where(lax.broadcasted_iota(jnp.int32, (KB, KB), 0)
                    == lax.broadcasted_iota(jnp.int32, (KB, KB), 1), 1.0, 0.0).astype(BF16)
    eye4 = jnp.concatenate([eye] * HEADS_PER_KV, axis=0)
    ags = []
    for g in range(N_KV_HEADS):
        qg = jnp.concatenate(
            [q_ref[:, (g * HEADS_PER_KV + r) * HEAD_DIM:(g * HEADS_PER_KV + r + 1) * HEAD_DIM]
             for r in range(HEADS_PER_KV)], axis=0)
        ags.append(jnp.concatenate([qg, eye4], axis=1))
    m_ref[...] = jnp.full(m_ref.shape, -inf, F32)
    acc_ref[...] = jnp.zeros(acc_ref.shape, F32)
    nrow = HEADS_PER_KV * KB
    part = nrow // ATTN_ROW_PARTS

    def qk(u):
        bias_t = bias_ref[rows(u), :]
        out = []
        for g in range(N_KV_HEADS):
            bmat = jnp.concatenate([kb_ref[rows(u), g * HEAD_DIM:(g + 1) * HEAD_DIM], bias_t],
                                   axis=1)
            out.append(lax.dot_general(ags[g], bmat, _NT,
                                       preferred_element_type=F32).astype(BF16))
        return tuple(out)

    def softmax_pv(u, ts):
        for g in range(N_KV_HEADS):
            vx = vext_ref[g, rows(u), :]
            for hp in range(ATTN_ROW_PARTS):
                rs = slice(hp * part, (hp + 1) * part)
                t = ts[g][rs]
                m_old = m_ref[g, rs]
                m_new = jnp.maximum(m_old, jnp.max(t, axis=1, keepdims=True).astype(F32))
                p = jnp.exp2(t - m_new.astype(BF16))
                acc_ref[g, rs] = (jnp.exp2(m_old - m_new) * acc_ref[g, rs]
                                  + jnp.dot(p, vx, preferred_element_type=F32))
                m_ref[g, rs] = m_new

    def pa(u, ts):
        nxt = qk(u + 1)
        softmax_pv(u, ts)
        return nxt

    softmax_pv(nsup - 1, lax.fori_loop(0, nsup - 1, pa, qk(0)))
    for g in range(N_KV_HEADS):
        acc = acc_ref[g]
        o = acc[:, :HEAD_DIM] / acc[:, HEAD_DIM:]
        for r in range(HEADS_PER_KV):
            h = g * HEADS_PER_KV + r
            o_ref[:, h * HEAD_DIM:(h + 1) * HEAD_DIM] = o[r * KB:(r + 1) * KB].astype(BF16)


def _attn_prompt(q, kb, vb, qi, kib, wi_t):
    B, S, _ = q.shape
    n_sel = min(TOPK_MAX, S // 4)
    qblk = lambda w: pl.BlockSpec((None, KB, w), lambda b, i: (b, i, 0))
    full = lambda w: pl.BlockSpec((None, S, w), lambda b, i: (b, 0, 0))
    nrow = HEADS_PER_KV * KB
    return pl.pallas_call(
        functools.partial(_attn_prompt_kernel, n_sel=n_sel),
        grid=(B, S // KB),
        in_specs=[qblk(Q_W), full(KV_W), full(KV_W), qblk(QI_W), full(IDX_DIM),
                  pl.BlockSpec((None, N_IDX_HEADS, KB), lambda b, i: (b, 0, i))],
        out_specs=qblk(Q_W),
        out_shape=jax.ShapeDtypeStruct((B, S, Q_W), BF16),
        scratch_shapes=[pltpu.VMEM((S, KB), F32), pltpu.VMEM((S, KB), BF16),
                        pltpu.VMEM((N_KV_HEADS, S, 2 * HEAD_DIM), BF16),
                        pltpu.VMEM((RANK_BLOCK, RANK_BLOCK), BF16),
                        pltpu.VMEM((N_KV_HEADS, nrow, 1), F32),
                        pltpu.VMEM((N_KV_HEADS, nrow, 2 * HEAD_DIM), F32)],
        compiler_params=_cparams(2),
        name="attn_prompt",
    )(q, kb, vb, qi, kib, wi_t)


def _idx_scores(qi, wcol, keys_t):
    s = jnp.dot(qi, keys_t, preferred_element_type=F32)
    r = jnp.maximum(s, 0.0) * wcol
    out = r[0:SUBLANES]
    for h in range(1, N_IDX_HEADS):
        out = out + r[h * SUBLANES:(h + 1) * SUBLANES]
    return out


def _score_sample_kernel(pt_ref, qi_ref, w_ref, *refs, past):
    ki_refs = refs[:IDX_PAGES_PER_STEP]
    kinew_ref, sc_ref = refs[IDX_PAGES_PER_STEP:]
    j = pl.program_id(1)
    qi = qi_ref[...]
    wcol = w_ref[:, 0:1]
    width = IDX_PAGES_PER_STEP * PAGE_SIZE
    keys_t = jnp.concatenate([r[...] for r in ki_refs], axis=1).astype(BF16)
    nq = sc_ref.shape[0]
    sc_ref[:, pl.ds(pl.multiple_of(j * width, width), width)] = _idx_scores(qi, wcol, keys_t)[:nq]

    @pl.when(j == pl.num_programs(1) - 1)
    def _():
        inf = jnp.float32(jnp.inf)
        row = lax.broadcasted_iota(jnp.int32, (SUBLANES, PAGE_SIZE), 0)
        lane = lax.broadcasted_iota(jnp.int32, (SUBLANES, PAGE_SIZE), 1)
        snew = _idx_scores(qi, wcol, kinew_ref[...])
        sc_ref[:, past:past + PAGE_SIZE] = jnp.where(lane <= row, snew, -inf)[:nq]
        sc_ref[:, past + PAGE_SIZE:] = jnp.full((nq, SUPER - PAGE_SIZE), -inf, F32)


def _score_sample(page_table, qi_pad, wcol, cache_ki_t, kinew_t, nq):
    B, n_pages = page_table.shape
    past = n_pages * PAGE_SIZE
    steps = n_pages // IDX_PAGES_PER_STEP
    page = lambda p: pl.BlockSpec((None, IDX_DIM, PAGE_SIZE),
                                  lambda b, j, pt: (pt[b, j * IDX_PAGES_PER_STEP + p], 0, 0))
    per_b = lambda r, w: pl.BlockSpec((None, r, w), lambda b, j, pt: (b, 0, 0))
    return pl.pallas_call(
        functools.partial(_score_sample_kernel, past=past),
        grid_spec=pltpu.PrefetchScalarGridSpec(
            num_scalar_prefetch=1,
            grid=(B, steps),
            in_specs=[per_b(N_IDX_HEADS * SUBLANES, IDX_DIM), per_b(N_IDX_HEADS * SUBLANES, LANES)]
                     + [page(p) for p in range(IDX_PAGES_PER_STEP)]
                     + [per_b(IDX_DIM, PAGE_SIZE)],
            out_specs=per_b(nq, past + SUPER),
        ),
        out_shape=jax.ShapeDtypeStruct((B, nq, past + SUPER), F32),
        compiler_params=_cparams(2),
        name="score_sample",
    )(page_table, qi_pad, wcol, *([cache_ki_t] * IDX_PAGES_PER_STEP), kinew_t)


def _lanes_reduce(x, op):
    out = x[:, :LANES]
    for t in range(1, x.shape[1] // LANES):
        out = op(out, x[:, t * LANES:(t + 1) * LANES])
    return out


def _select_sample_kernel(sc_ref, sel_ref, tri_ref, spread_ref, *, n_sel):
    R, L = sc_ref.shape
    nsup = L // SUPER
    kf = float(n_sel)
    inf = jnp.float32(jnp.inf)
    r = lax.broadcasted_iota(jnp.int32, (SUPER, SUPER), 0)
    c = lax.broadcasted_iota(jnp.int32, (SUPER, SUPER), 1)
    tri_ref[...] = jnp.where(r <= c, 1.0, 0.0).astype(BF16)

    def cols(u):
        return pl.ds(pl.multiple_of(u * SUPER, SUPER), SUPER)

    def sweep(fn, init):
        return lax.fori_loop(0, nsup, lambda u, cr: fn(sc_ref[:, cols(u)], cr), init)

    zr = jnp.zeros((R, LANES), F32)
    rsum = lambda a: jnp.sum(a, axis=1, keepdims=True)
    rmax = lambda a: jnp.max(a, axis=1, keepdims=True)
    rmin = lambda a: jnp.min(a, axis=1, keepdims=True)
    one = lambda m: jnp.where(m, 1.0, 0.0)

    def f0(x, cr):
        fin = x > -inf
        return (jnp.minimum(cr[0], _lanes_reduce(jnp.where(fin, x, inf), jnp.minimum)),
                jnp.maximum(cr[1], _lanes_reduce(x, jnp.maximum)),
                cr[2] + _lanes_reduce(one(fin), jnp.add))
    lo8, hi8, n8 = sweep(f0, (zr + inf, zr - inf, zr))

    def count_ge(p):
        return rsum(sweep(lambda x, cr: cr + _lanes_reduce(one(x >= p), jnp.add), zr))

    def bracket(lo, hi):
        def f(x, cr):
            return (jnp.minimum(cr[0], _lanes_reduce(jnp.where(x >= lo, x, inf), jnp.minimum)),
                    jnp.maximum(cr[1], _lanes_reduce(jnp.where(x <= hi, x, -inf), jnp.maximum)))
        a8, b8 = sweep(f, (zr + inf, zr - inf))
        return rmin(a8), rmax(b8)

    def scan(p):
        def f(x, cr):
            ge = x >= p
            return (cr[0] + _lanes_reduce(one(ge), jnp.add),
                    jnp.maximum(cr[1], _lanes_reduce(jnp.where(ge, -inf, x), jnp.maximum)),
                    jnp.minimum(cr[2], _lanes_reduce(jnp.where(ge, x, inf), jnp.minimum)))
        cnt, lo, hi = sweep(f, (zr, zr - inf, zr + inf))
        return rsum(cnt), rmax(lo), rmin(hi)

    thr = _search_threshold(count_ge, bracket, scan, rmin(lo8), rmax(hi8), rsum(n8), kf)
    need = kf - rsum(sweep(lambda x, cr: cr + _lanes_reduce(one(x > thr), jnp.add), zr))

    wide = N_KV_HEADS * SUPER
    spread_ref[...] = jnp.where(
        lax.broadcasted_iota(jnp.int32, (SUPER, wide), 1) // N_KV_HEADS
        == lax.broadcasted_iota(jnp.int32, (SUPER, wide), 0), 1.0, 0.0).astype(BF16)

    def p3(u, run):
        x = sc_ref[:, cols(u)]
        eq = x == thr
        rank = jnp.dot(one(eq).astype(BF16), tri_ref[...], preferred_element_type=F32) + run
        sel = jnp.where(x > thr, 1.0, jnp.where(eq, one(rank <= need), 0.0))
        sel_ref[:, pl.ds(pl.multiple_of(u * wide, wide), wide)] = jnp.dot(
            sel.astype(BF16), spread_ref[...], preferred_element_type=F32)
        return rank[:, SUPER - 1:SUPER]

    lax.fori_loop(0, nsup, p3, jnp.zeros((R, 1), F32))


def _select_sample(sc, n_sel):
    R, L = sc.shape
    return pl.pallas_call(
        functools.partial(_select_sample_kernel, n_sel=n_sel),
        grid=(R // SEL_ROWS,),
        in_specs=[pl.BlockSpec((SEL_ROWS, L), lambda i: (i, 0))],
        out_specs=pl.BlockSpec((SEL_ROWS, N_KV_HEADS * L), lambda i: (i, 0)),
        out_shape=jax.ShapeDtypeStruct((R, N_KV_HEADS * L), F32),
        scratch_shapes=[pltpu.VMEM((SUPER, SUPER), BF16),
                        pltpu.VMEM((SUPER, N_KV_HEADS * SUPER), BF16)],
        compiler_params=_cparams(1),
        name="select_sample",
    )(sc)


def _attn_sample_kernel(pt_ref, q_ref, sel_ref, *refs, past):
    k_refs = refs[:PAGES_PER_STEP]
    v_refs = refs[PAGES_PER_STEP:2 * PAGES_PER_STEP]
    knew_ref, vnew_ref, o_ref, m_ref, acc_ref = refs[2 * PAGES_PER_STEP:]
    j = pl.program_id(1)
    nrow = N_HEADS * SUBLANES
    pw = N_KV_HEADS * PAGE_SIZE
    per_chain = PAGES_PER_STEP // SAMPLE_CHAINS

    @pl.when(j == 0)
    def _():
        m_ref[...] = jnp.full(m_ref.shape, -jnp.inf, F32)
        acc_ref[...] = jnp.zeros(acc_ref.shape, F32)

    q = q_ref[...]

    def update(ch, kk, vv, col):
        n = kk.shape[0]
        s = lax.dot_general(q, kk, _NT, preferred_element_type=F32)
        sel = jnp.concatenate([sel_ref[:, pl.ds(col, n)]] * N_HEADS, axis=0)
        own = (lax.broadcasted_iota(jnp.int32, (nrow, n), 0) // (HEADS_PER_KV * SUBLANES)
               == lax.broadcasted_iota(jnp.int32, (nrow, n), 1) % N_KV_HEADS)
        s = jnp.where(own, jnp.where(sel > 0.5, s, MASK_BIAS), MASK_BIAS)
        m_old = m_ref[ch]
        m_new = jnp.maximum(m_old, jnp.max(s, axis=1, keepdims=True))
        pb = jnp.exp2(s - m_new).astype(BF16)
        vext = jnp.concatenate([vv, jnp.ones((n, HEAD_DIM), BF16)], axis=1)
        acc_ref[ch] = (jnp.exp2(m_old - m_new) * acc_ref[ch]
                       + jnp.dot(pb, vext, preferred_element_type=F32))
        m_ref[ch] = m_new

    for ch in range(SAMPLE_CHAINS):
        pages = range(ch * per_chain, (ch + 1) * per_chain)
        kk = jnp.concatenate([k_refs[p][...].astype(BF16) for p in pages], axis=0)
        vv = jnp.concatenate([v_refs[p][...].astype(BF16) for p in pages], axis=0)
        col = pl.multiple_of((j * PAGES_PER_STEP + ch * per_chain) * pw, per_chain * pw)
        update(ch, kk, vv, col)

    @pl.when(j == pl.num_programs(1) - 1)
    def _():
        update(0, knew_ref[...], vnew_ref[...], N_KV_HEADS * past)
        m = m_ref[0]
        for ch in range(1, SAMPLE_CHAINS):
            m = jnp.maximum(m, m_ref[ch])
        acc = jnp.exp2(m_ref[0] - m) * acc_ref[0]
        for ch in range(1, SAMPLE_CHAINS):
            acc = acc + jnp.exp2(m_ref[ch] - m) * acc_ref[ch]
        o_ref[...] = acc[:, :HEAD_DIM] / acc[:, HEAD_DIM:]


def _attn_sample(page_table, q_all, sel, cache_k, cache_v, knew, vnew):
    B, n_pages = page_table.shape
    past = n_pages * PAGE_SIZE
    steps = n_pages // PAGES_PER_STEP
    nrow = N_HEADS * SUBLANES
    pw = N_KV_HEADS * PAGE_SIZE
    page = lambda p: pl.BlockSpec((pw, HEAD_DIM), lambda b, j, pt: (pt[b, j * PAGES_PER_STEP + p], 0))
    per_b = lambda r, w: pl.BlockSpec((None, r, w), lambda b, j, pt: (b, 0, 0))
    return pl.pallas_call(
        functools.partial(_attn_sample_kernel, past=past),
        grid_spec=pltpu.PrefetchScalarGridSpec(
            num_scalar_prefetch=1,
            grid=(B, steps),
            in_specs=[per_b(nrow, HEAD_DIM), per_b(SUBLANES, sel.shape[2])]
                     + [page(p) for p in range(PAGES_PER_STEP)] * 2
                     + [per_b(pw, HEAD_DIM), per_b(pw, HEAD_DIM)],
            out_specs=per_b(nrow, HEAD_DIM),
            scratch_shapes=[pltpu.VMEM((SAMPLE_CHAINS, nrow, 1), F32),
                            pltpu.VMEM((SAMPLE_CHAINS, nrow, 2 * HEAD_DIM), F32)],
        ),
        out_shape=jax.ShapeDtypeStruct((B, nrow, HEAD_DIM), F32),
        compiler_params=_cparams(2),
        name="attn_sample",
    )(page_table, q_all, sel, *([cache_k] * PAGES_PER_STEP), *([cache_v] * PAGES_PER_STEP), knew, vnew)


def _ffn_kernel(*refs, sample, final):
    it = iter(refs)
    h_ref, o_ref, wout_ref, g_ref, wup_ref, cw_ref, cb_ref, wdown_ref = (next(it) for _ in range(8))
    gfin_ref = next(it) if final else None
    st1_ref, st2_ref = (next(it), next(it)) if sample else (None, None)
    hout_ref, cst_ref = next(it), next(it)
    act_ref = next(it)
    carry_ref = None if sample else next(it)

    tm = h_ref.shape[0]
    h1 = h_ref[...] + jnp.dot(o_ref[...], wout_ref[...], preferred_element_type=F32)
    xn = _rmsnorm(h1, g_ref[...]).astype(BF16)

    if sample:
        t4 = lax.broadcasted_iota(jnp.int32, (tm, FF_CHUNK), 0) % 4
    else:
        @pl.when(pl.program_id(1) == 0)
        def _():
            carry_ref[...] = jnp.zeros(carry_ref.shape, F32)
        row8 = lax.broadcasted_iota(jnp.int32, (SUBLANES, FF_CHUNK), 0)

    def conv(cols):
        u = jnp.dot(xn, wup_ref[:, cols], preferred_element_type=F32)
        r1 = pltpu.roll(u, 1, 0)
        r2 = pltpu.roll(u, 2, 0)
        if sample:
            s1 = jnp.where(t4 == 0, st1_ref[:, cols], r1)
            s2 = jnp.where(t4 == 0, st2_ref[:, cols], jnp.where(t4 == 1, st1_ref[:, cols], r2))
            cst_ref[:, cols] = u
        else:
            prev = carry_ref[:, cols]
            top1 = jnp.where(row8 < 1, pltpu.roll(prev, 1, 0), r1[:SUBLANES])
            top2 = jnp.where(row8 < 2, pltpu.roll(prev, 2, 0), r2[:SUBLANES])
            s1 = jnp.concatenate([top1, r1[SUBLANES:]], axis=0)
            s2 = jnp.concatenate([top2, r2[SUBLANES:]], axis=0)
            carry_ref[:, cols] = u[tm - SUBLANES:]
            cst_ref[:, cols] = u[tm - SUBLANES:]
        cw = cw_ref[:, cols]
        return cb_ref[:, cols] + cw[0:1] * s2 + cw[1:2] * s1 + cw[2:3] * u

    for c in range(N_FF_CHUNKS):
        gate = conv(slice(c * FF_CHUNK, (c + 1) * FF_CHUNK))
        val = conv(slice(D_FF + c * FF_CHUNK, D_FF + (c + 1) * FF_CHUNK))
        act_ref[:, c * FF_CHUNK:(c + 1) * FF_CHUNK] = (
            gate * (1.0 / (1.0 + jnp.exp(-gate))) * val).astype(BF16)

    out = h1 + jnp.dot(act_ref[...], wdown_ref[...], preferred_element_type=F32)
    if final:
        out = _rmsnorm(out, gfin_ref[...])
    hout_ref[...] = out


def _ffn(h, o, wout, g, wup, cw, cb, wdown, gfin, state, tm):
    B, T, _ = h.shape
    sample = state is not None
    final = gfin is not None
    row = lambda b, t: (b, t, 0)
    const = lambda b, t: (0, 0)
    w = lambda a: pl.BlockSpec(a.shape, const, pipeline_mode=pl.Buffered(1))
    operands = [h, o, wout, g, wup, cw, cb, wdown]
    in_specs = [pl.BlockSpec((None, tm, D_MODEL), row), pl.BlockSpec((None, tm, D_MODEL), row),
                w(wout), w(g), w(wup), w(cw), w(cb), w(wdown)]
    if final:
        operands.append(gfin)
        in_specs.append(w(gfin))
    scratch = [pltpu.VMEM((tm, D_FF), BF16)]
    if sample:
        operands += [state[0], state[1]]
        in_specs += [w(state[0]), w(state[1])]
        cst_shape = jax.ShapeDtypeStruct((B, T, 2 * D_FF), F32)
        cst_spec = pl.BlockSpec((None, tm, 2 * D_FF), row)
    else:
        cst_shape = jax.ShapeDtypeStruct((B, SUBLANES, 2 * D_FF), F32)
        cst_spec = pl.BlockSpec((None, SUBLANES, 2 * D_FF), lambda b, t: (b, 0, 0))
        scratch.append(pltpu.VMEM((SUBLANES, 2 * D_FF), F32))
    return pl.pallas_call(
        functools.partial(_ffn_kernel, sample=sample, final=final),
        grid=(B, T // tm),
        in_specs=in_specs,
        out_specs=[pl.BlockSpec((None, tm, D_MODEL), row), cst_spec],
        out_shape=[jax.ShapeDtypeStruct((B, T, D_MODEL), F32), cst_shape],
        scratch_shapes=scratch,
        compiler_params=_cparams(2),
        name="ffn_sample" if sample else "ffn_prompt",
    )(*operands)


def _sgu_kernel(h_ref, g_ref, win_ref, lng_ref, lnb_ref, ws_ref, bias_ref, *out_refs, emit_v):
    y_ref = out_refs[0]
    tm = h_ref.shape[0]
    xn = _rmsnorm(h_ref[...], g_ref[...]).astype(BF16)
    uv = jnp.dot(xn, win_ref[...], preferred_element_type=F32)
    u = uv[:, :D_MODEL]
    v = uv[:, D_MODEL:]
    xc = v - jnp.mean(v, axis=-1, keepdims=True)
    vn = xc * lax.rsqrt(jnp.mean(xc * xc, axis=-1, keepdims=True) + LN_EPS) * lng_ref[...] + lnb_ref[...]
    if emit_v:
        out_refs[1][...] = vn
    vnb = vn.astype(BF16)
    tri = (lax.broadcasted_iota(jnp.int32, (CHUNK, CHUNK), 0)
           >= lax.broadcasted_iota(jnp.int32, (CHUNK, CHUNK), 1))
    for g in range(SGU_GROUPS):
        wg = jnp.where(tri, ws_ref[g], 0.0).astype(BF16)
        cols = slice(g * SGU_GROUP_DIM, (g + 1) * SGU_GROUP_DIM)
        for c in range(tm // CHUNK):
            rows = slice(c * CHUNK, (c + 1) * CHUNK)
            s = jnp.dot(wg, vnb[rows, cols], preferred_element_type=F32) + bias_ref[:, cols]
            y_ref[rows, cols] = (u[rows, cols] * s).astype(BF16)


def _sgu_bias(bs):
    return jnp.repeat(bs.T, SGU_GROUP_DIM, axis=1)


def _sgu_sample_weights(ws, bs, B, T):
    corner = ws[:, :T, :T]
    ws_s = jnp.einsum('bc,gts->gbtcs', jnp.eye(B, dtype=ws.dtype), corner)
    return ws_s.reshape(SGU_GROUPS, B * T, B * T), jnp.tile(_sgu_bias(bs)[:T], (B, 1))


def _sgu(h, g, win, lng, lnb, ws, bias, tm, emit_v):
    B, T, _ = h.shape
    row = lambda b, t: (b, t, 0)
    w = lambda a: pl.BlockSpec(a.shape, lambda b, t: (0,) * a.ndim)
    out_shape = [jax.ShapeDtypeStruct((B, T, D_MODEL), BF16)]
    out_specs = [pl.BlockSpec((None, tm, D_MODEL), row)]
    if emit_v:
        out_shape.append(jax.ShapeDtypeStruct((B, T, D_MODEL), F32))
        out_specs.append(pl.BlockSpec((None, tm, D_MODEL), row))
    return pl.pallas_call(
        functools.partial(_sgu_kernel, emit_v=emit_v),
        grid=(B, T // tm),
        in_specs=[pl.BlockSpec((None, tm, D_MODEL), row), w(g), w(win), w(lng), w(lnb), w(ws), w(bias)],
        out_specs=out_specs,
        out_shape=out_shape,
        compiler_params=_cparams(2),
        name="sgu",
    )(h, g, win, lng, lnb, ws, bias)


def _pad_axis(a, axis, size):
    pad = [(0, 0)] * a.ndim
    pad[axis] = (0, size - a.shape[axis])
    return jnp.pad(a, pad)


def _sample_attention(qs, ksb, vsb, qis, kws, kisb, page_table, cache_k, cache_v, cache_ki, Bs, Ts):
    Ns = Bs * Ts
    n_pool = cache_k.shape[1]
    past = page_table.shape[1] * PAGE_SIZE
    qi_pad = _pad_axis(qis.reshape(Bs, Ts, N_IDX_HEADS, IDX_DIM).transpose(0, 2, 1, 3), 2, SUBLANES)
    qi_pad = qi_pad.reshape(Bs, N_IDX_HEADS * SUBLANES, IDX_DIM)
    wcol = kws[0, :, IDX_DIM:IDX_DIM + N_IDX_HEADS].reshape(Bs, Ts, N_IDX_HEADS).transpose(0, 2, 1)
    wcol = _pad_axis(wcol, 2, SUBLANES).reshape(Bs, N_IDX_HEADS * SUBLANES, 1)
    wcol = jnp.broadcast_to(wcol, (Bs, N_IDX_HEADS * SUBLANES, LANES))
    kinew_t = jnp.swapaxes(_pad_axis(kisb.reshape(Bs, Ts, IDX_DIM), 1, PAGE_SIZE), 1, 2)
    cache_ki_t = jnp.swapaxes(cache_ki, 2, 3).reshape(n_pool, IDX_DIM, PAGE_SIZE)
    sc = _score_sample(page_table, qi_pad, wcol, cache_ki_t, kinew_t, Ts)
    n_sel = min(TOPK_MAX, (past + Ts) // 4)
    sel = _select_sample(sc.reshape(Ns, past + SUPER), n_sel)
    sel = _pad_axis(sel.reshape(Bs, Ts, N_KV_HEADS * (past + SUPER)), 1, SUBLANES)
    q_all = qs.reshape(Bs, Ts, N_HEADS, HEAD_DIM).transpose(0, 2, 1, 3)
    q_all = _pad_axis(q_all, 2, SUBLANES).reshape(Bs, N_HEADS * SUBLANES, HEAD_DIM)
    pw = N_KV_HEADS * PAGE_SIZE
    knew = _pad_axis(ksb.reshape(Bs, Ts * N_KV_HEADS, HEAD_DIM), 1, pw)
    vnew = _pad_axis(vsb.reshape(Bs, Ts * N_KV_HEADS, HEAD_DIM), 1, pw)
    o_s = _attn_sample(page_table, q_all, sel, cache_k.reshape(n_pool * pw, HEAD_DIM),
                       cache_v.reshape(n_pool * pw, HEAD_DIM), knew, vnew)
    o_s = o_s.reshape(Bs, N_HEADS, SUBLANES, HEAD_DIM)[:, :, :Ts]
    return o_s.transpose(0, 2, 1, 3).reshape(1, Ns, Q_W).astype(BF16)


def kernel(x_prompt, x_sample, cache_k, cache_v, cache_idx_k, state_ffn_conv, page_table, attn_norm,
           w_attn_in, w_attn_out, sgu_norm, w_sgu_in, sgu_ln_g, sgu_ln_b, sgu_ws, sgu_bs, w_sgu_out,
           ffn_norm, w_ffn_up, ffn_conv_w, ffn_conv_b, w_ffn_down, final_norm):
    B, S, _ = x_prompt.shape
    Bs, Ts, _ = x_sample.shape
    Ns = Bs * Ts
    n_pages = page_table.shape[1]
    past = n_pages * PAGE_SIZE
    assert w_ffn_up.shape[0] == 2 and w_attn_in.shape[0] == 1 and w_sgu_in.shape[0] == 1
    assert Ns == CHUNK and Ts == 4 and S % SUPER == 0
    assert n_pages % PAGES_PER_STEP == 0 and n_pages % IDX_PAGES_PER_STEP == 0 and Ns % SEL_ROWS == 0
    bf = lambda a: a.astype(BF16)
    r1 = lambda a: a[None]
    tm_p, tm_s = 512, Ns
    tm_ffn = FFN_ROWS if S % FFN_ROWS == 0 else tm_p

    w_in = bf(_pad_axis(w_attn_in[0], 1, ATTN_IN_PAD))
    g_attn = r1(attn_norm[0])
    q, k32, v32, kb, vb, qi, kw, kib = _attn_in(x_prompt, jnp.arange(S), g_attn, w_in, tm_p)
    wi_t = jnp.swapaxes(kw[..., IDX_DIM:IDX_DIM + N_IDX_HEADS], 1, 2)
    o_p = _attn_prompt(q, kb, vb, qi, kib, wi_t)

    xs = x_sample.reshape(1, Ns, D_MODEL)
    pos_s = past + (jnp.arange(Ns) % Ts)
    qs, ks32, vs32, ksb, vsb, qis, kws, kisb = _attn_in(xs, pos_s, g_attn, w_in, tm_s)
    o_s = _sample_attention(qs, ksb, vsb, qis, kws, kisb, page_table,
                            cache_k[0:1], cache_v[0:1], cache_idx_k[0:1], Bs, Ts)

    def ffn(i, h, o, wout, gfin, sample):
        state = None
        if sample:
            st = state_ffn_conv[i]
            state = (jnp.repeat(st[:, 1], Ts, axis=0), jnp.repeat(st[:, 0], Ts, axis=0))
        return _ffn(h, o, bf(wout), r1(ffn_norm[i]), bf(w_ffn_up[i]), ffn_conv_w[i],
                    r1(ffn_conv_b[i]), bf(w_ffn_down[i]), gfin, state, tm_s if sample else tm_ffn)

    hp, cst_p0 = ffn(0, x_prompt, o_p, w_attn_out[0], None, False)
    hs, up_s0 = ffn(0, xs, o_s, w_attn_out[0], None, True)

    sgu_args = (r1(sgu_norm[0]), bf(w_sgu_in[0]), r1(sgu_ln_g[0]), r1(sgu_ln_b[0]))
    (y_p,) = _sgu(hp, *sgu_args, sgu_ws[0], _sgu_bias(sgu_bs[0]), tm_p, False)
    ws_s, bias_s = _sgu_sample_weights(sgu_ws[0], sgu_bs[0], Bs, Ts)
    y_s, vn_s = _sgu(hs, *sgu_args, ws_s, bias_s, tm_s, True)

    gfin = r1(final_norm)
    yp, cst_p1 = ffn(1, hp, y_p, w_sgu_out[0], gfin, False)
    ys, up_s1 = ffn(1, hs, y_s, w_sgu_out[0], gfin, True)

    tail = lambda c: c[:, SUBLANES - (CONV_W - 1):]
    stail = lambda u: u.reshape(Bs, Ts, 2 * D_FF)[:, Ts - (CONV_W - 1):]
    return (yp, ys.reshape(Bs, Ts, D_MODEL),
            k32.reshape(1, B, S, N_KV_HEADS, HEAD_DIM), v32.reshape(1, B, S, N_KV_HEADS, HEAD_DIM),
            kw[None, ..., :IDX_DIM],
            ks32.reshape(1, Bs, Ts, N_KV_HEADS, HEAD_DIM), vs32.reshape(1, Bs, Ts, N_KV_HEADS, HEAD_DIM),
            kws[..., :IDX_DIM].reshape(1, Bs, Ts, IDX_DIM),
            vn_s.reshape(1, Bs, Ts, D_MODEL),
            jnp.stack([tail(cst_p0), tail(cst_p1)]),
            jnp.stack([stail(up_s0), stail(up_s1)]))
```

```python
import functools

import numpy as np
import jax
import jax.numpy as jnp
from jax import lax
from jax.experimental import pallas as pl
from jax.experimental.pallas import tpu as pltpu

F32 = jnp.float32
BF16 = jnp.bfloat16

D_MODEL = 1024
N_HEADS = 8
HEAD_DIM = 128
N_KV_HEADS = 2
HEADS_PER_KV = N_HEADS // N_KV_HEADS
ROT_DIM = 32
ROPE_THETA = 500000.0
N_IDX_HEADS = 4
IDX_DIM = 64
IDX_ROT_DIM = 16
TOPK_MAX = 256
PAGE_SIZE = 128
Q_W = N_HEADS * HEAD_DIM
KV_W = N_KV_HEADS * HEAD_DIM
QI_W = N_IDX_HEADS * IDX_DIM
ATTN_IN = Q_W + 2 * KV_W + QI_W + IDX_DIM + N_IDX_HEADS
ATTN_IN_PAD = 1920
KW_COL = Q_W + 2 * KV_W + QI_W
CHUNK = 128
SGU_GROUPS = 4
SGU_GROUP_DIM = D_MODEL // SGU_GROUPS
D_FF = 2816
CONV_W = 3
NORM_EPS = 1e-6
LN_EPS = 1e-5
LOG2E = 1.4426950408889634
ATTN_SCALE_LOG2 = HEAD_DIM ** -0.5 * LOG2E
W_IDX_SCALE = (N_IDX_HEADS ** -0.5) * (IDX_DIM ** -0.5)

LANES = 128
SUBLANES = 8
KB = 128
SUPER = 512
ATTN_ROW_PARTS = 2
RANK_BLOCK = 256
FF_CHUNK = 256
PROMPT_ROWS = 1024
N_FF_CHUNKS = D_FF // FF_CHUNK
PAGES_PER_STEP = 16
IDX_PAGES_PER_STEP = 32
SAMPLE_CHAINS = 2
SEL_ROWS = 64
N_BISECT = 20
VMEM_LIMIT = 56 * 1024 * 1024
MASK_BIAS = -1e30

_NT = (((1,), (1,)), ((), ()))


def _cparams(n_axes):
    return pltpu.CompilerParams(
        dimension_semantics=("arbitrary",) * n_axes, vmem_limit_bytes=VMEM_LIMIT)


def _rmsnorm(x, g):
    return x * lax.rsqrt(jnp.mean(x * x, axis=-1, keepdims=True) + NORM_EPS) * g


def _rope(u, c, sa, sb, half):
    return u * c + pltpu.roll(u, LANES - half, 1) * sa + pltpu.roll(u, half, 1) * sb


def _attn_in_kernel(x_ref, g_ref, w_ref, ch_ref, sha_ref, shb_ref, ci_ref, sia_ref, sib_ref,
                    ckw_ref, skwa_ref, skwb_ref,
                    q_ref, k_ref, v_ref, kb_ref, vb_ref, qi_ref, kw_ref, kib_ref):
    xn = _rmsnorm(x_ref[...], g_ref[...]).astype(BF16)
    z = jnp.dot(xn, w_ref[...], preferred_element_type=F32)
    ch, sha, shb = ch_ref[...], sha_ref[...], shb_ref[...]
    for h in range(N_HEADS):
        sl = slice(h * HEAD_DIM, (h + 1) * HEAD_DIM)
        q_ref[:, sl] = (_rope(z[:, sl], ch, sha, shb, ROT_DIM // 2) * ATTN_SCALE_LOG2).astype(BF16)
    for h in range(N_KV_HEADS):
        sl = slice(h * HEAD_DIM, (h + 1) * HEAD_DIM)
        kk = _rope(z[:, Q_W + h * HEAD_DIM:Q_W + (h + 1) * HEAD_DIM], ch, sha, shb, ROT_DIM // 2)
        k_ref[pl.ds(h, kk.shape[0], stride=N_KV_HEADS), :] = kk
        kb_ref[:, sl] = kk.astype(BF16)
    v = z[:, Q_W + KV_W:Q_W + 2 * KV_W]
    for h in range(N_KV_HEADS):
        v_ref[pl.ds(h, v.shape[0], stride=N_KV_HEADS), :] = v[:, h * HEAD_DIM:(h + 1) * HEAD_DIM]
    vb_ref[...] = v.astype(BF16)
    ci, sia, sib = ci_ref[...], sia_ref[...], sib_ref[...]
    qi0 = Q_W + 2 * KV_W
    for s in range(QI_W // LANES):
        sl = slice(s * LANES, (s + 1) * LANES)
        qi_ref[:, sl] = _rope(z[:, qi0 + s * LANES:qi0 + (s + 1) * LANES], ci, sia, sib,
                              IDX_ROT_DIM // 2).astype(BF16)
    kw = _rope(z[:, KW_COL:KW_COL + LANES], ckw_ref[...], skwa_ref[...], skwb_ref[...],
               IDX_ROT_DIM // 2)
    kw_ref[...] = kw
    kib_ref[...] = kw[:, :IDX_DIM].astype(BF16)


def _rope_tables(pos, rot, period, tail_scale=None):
    half = rot // 2
    freqs = ROPE_THETA ** (-(jnp.arange(half, dtype=F32) * 2.0) / rot)
    ang = pos.astype(F32)[:, None] * freqs[None, :]
    cos, sin = jnp.cos(ang), jnp.sin(ang)
    lane = np.arange(LANES) % period
    col = lane % half
    c = jnp.where(lane < rot, cos[:, col], 1.0)
    sa = jnp.where(lane < half, -sin[:, col], 0.0)
    sb = jnp.where((lane >= half) & (lane < rot), sin[:, col], 0.0)
    if tail_scale is not None:
        glane = np.arange(LANES)
        c = jnp.where(glane < period, c, 1.0)
        c = jnp.where((glane >= IDX_DIM) & (glane < IDX_DIM + N_IDX_HEADS), tail_scale, c)
        sa = jnp.where(glane < period, sa, 0.0)
        sb = jnp.where(glane < period, sb, 0.0)
    return c.astype(F32), sa.astype(F32), sb.astype(F32)


def _attn_in(x, pos, g, w_pad, tm):
    B, T, _ = x.shape
    ch, sha, shb = _rope_tables(pos, ROT_DIM, HEAD_DIM)
    ci, sia, sib = _rope_tables(pos, IDX_ROT_DIM, IDX_DIM)
    ckw, skwa, skwb = _rope_tables(pos, IDX_ROT_DIM, IDX_DIM, tail_scale=W_IDX_SCALE)
    row = lambda b, t: (b, t, 0)
    tab = lambda b, t: (t, 0)
    const = lambda b, t: (0, 0)
    blk = lambda w: pl.BlockSpec((None, tm, w), row)
    tspec = pl.BlockSpec((tm, LANES), tab)
    kvrows = pl.BlockSpec((None, tm * N_KV_HEADS, HEAD_DIM), row)
    outs = pl.pallas_call(
        _attn_in_kernel,
        grid=(B, T // tm),
        in_specs=[blk(D_MODEL), pl.BlockSpec((1, D_MODEL), const),
                  pl.BlockSpec((D_MODEL, ATTN_IN_PAD), const),
                  tspec, tspec, tspec, tspec, tspec, tspec, tspec, tspec, tspec],
        out_specs=[blk(Q_W), kvrows, kvrows, blk(KV_W), blk(KV_W), blk(QI_W),
                   blk(LANES), blk(IDX_DIM)],
        out_shape=[jax.ShapeDtypeStruct((B, T, Q_W), BF16),
                   jax.ShapeDtypeStruct((B, T * N_KV_HEADS, HEAD_DIM), F32),
                   jax.ShapeDtypeStruct((B, T * N_KV_HEADS, HEAD_DIM), F32),
                   jax.ShapeDtypeStruct((B, T, KV_W), BF16),
                   jax.ShapeDtypeStruct((B, T, KV_W), BF16),
                   jax.ShapeDtypeStruct((B, T, QI_W), BF16),
                   jax.ShapeDtypeStruct((B, T, LANES), F32),
                   jax.ShapeDtypeStruct((B, T, IDX_DIM), BF16)],
        compiler_params=_cparams(2),
        name="attn_in",
    )(x, g, w_pad, ch, sha, shb, ci, sia, sib, ckw, skwa, skwb)
    return outs


def _search_threshold(count_ge, bracket, scan, lo0, hi0, n0, kf):
    hi0 = jnp.where(n0 <= kf, lo0, hi0)

    def coarse(_, st):
        lo, hi = st
        p = 0.5 * lo + 0.5 * hi
        ge = count_ge(p) >= kf
        return jnp.where(ge, p, lo), jnp.where(ge, hi, p)

    lo, hi = lax.fori_loop(0, N_BISECT, coarse, (lo0, hi0))
    a, b = bracket(lo, hi)

    def n_open(a, b):
        return jnp.max(jnp.where(a < b, 1, 0).astype(jnp.int32))

    def body(carry):
        _, a, b = carry
        p = jnp.minimum(jnp.maximum(0.5 * a + 0.5 * b, a), b)
        p = jnp.where(p > a, p, b)
        cnt, lowmax, highmin = scan(p)
        act = a < b
        a2 = jnp.where(act, jnp.where(cnt >= kf, highmin, a), a)
        b2 = jnp.where(act, jnp.where(cnt >= kf, b, lowmax), b)
        return n_open(a2, b2), a2, b2

    out = lax.while_loop(lambda c: c[0] > 0, body, (n_open(a, b), a, b))
    return out[1]


ACC_ROWS = 64


def _rows_reduce8(x, op):
    return op(x.reshape(x.shape[0] // ACC_ROWS, ACC_ROWS, LANES), axis=0)


def _attn_prompt_kernel(q_ref, kb_ref, vb_ref, qi_ref, kib_ref, wi_ref, o_ref,
                        sc_ref, bias_ref, vext_ref, tri_ref, m_ref, acc_ref, *, n_sel):
    i = pl.program_id(1)
    nsup = i // (SUPER // KB) + 1
    kf = float(n_sel)
    inf = jnp.float32(jnp.inf)

    @pl.when(i == 0)
    def _():
        for g in range(N_KV_HEADS):
            vext_ref[g, :, :HEAD_DIM] = vb_ref[:, g * HEAD_DIM:(g + 1) * HEAD_DIM]
            vext_ref[g, :, HEAD_DIM:] = jnp.ones((vext_ref.shape[1], HEAD_DIM), BF16)
        r = lax.broadcasted_iota(jnp.int32, (RANK_BLOCK, RANK_BLOCK), 0)
        c = lax.broadcasted_iota(jnp.int32, (RANK_BLOCK, RANK_BLOCK), 1)
        tri_ref[...] = jnp.where(c <= r, 1.0, 0.0).astype(BF16)

    def rows(u):
        return pl.ds(pl.multiple_of(u * SUPER, SUPER), SUPER)

    def sweep(fn, init):
        return lax.fori_loop(0, nsup, lambda u, c: fn(sc_ref[rows(u), :], c), init)

    z8 = jnp.zeros((ACC_ROWS, LANES), F32)
    fold = lambda acc8, op: op(acc8, axis=0, keepdims=True)

    w = wi_ref[...]
    qi = qi_ref[...]
    qi_rows = jnp.concatenate([qi[:, h * IDX_DIM:(h + 1) * IDX_DIM] for h in range(N_IDX_HEADS)],
                              axis=0)
    kmq = (lax.broadcasted_iota(jnp.int32, (SUPER, KB), 0)
           - lax.broadcasted_iota(jnp.int32, (SUPER, KB), 1))

    def p1(u, carry):
        amin, amax = carry
        kic = kib_ref[rows(u), :]
        sh = lax.dot_general(kic, qi_rows, _NT, preferred_element_type=F32)
        s = None
        for h in range(N_IDX_HEADS):
            t = w[h:h + 1, :] * jnp.maximum(sh[:, h * KB:(h + 1) * KB], 0.0)
            s = t if s is None else s + t
        causal = kmq <= i * KB - u * SUPER
        x = jnp.where(causal, s, -inf)
        sc_ref[rows(u), :] = x
        amin = jnp.minimum(amin, _rows_reduce8(jnp.where(causal, s, inf), jnp.min))
        amax = jnp.maximum(amax, _rows_reduce8(x, jnp.max))
        return amin, amax

    amin8, amax8 = lax.fori_loop(0, nsup, p1, (z8 + inf, z8 - inf))
    lane = lax.broadcasted_iota(jnp.int32, (1, LANES), 1)
    n0 = (i * KB + lane + 1).astype(F32)

    def count_ge(p):
        return fold(sweep(lambda x, c: c + _rows_reduce8(jnp.where(x >= p, 1.0, 0.0), jnp.sum), z8),
                    jnp.sum)

    def bracket(lo, hi):
        def f(x, c):
            return (jnp.minimum(c[0], _rows_reduce8(jnp.where(x >= lo, x, inf), jnp.min)),
                    jnp.maximum(c[1], _rows_reduce8(jnp.where(x <= hi, x, -inf), jnp.max)))
        a8, b8 = sweep(f, (z8 + inf, z8 - inf))
        return fold(a8, jnp.min), fold(b8, jnp.max)

    def scan(p):
        def f(x, c):
            ge = x >= p
            return (c[0] + _rows_reduce8(jnp.where(ge, 1.0, 0.0), jnp.sum),
                    jnp.maximum(c[1], _rows_reduce8(jnp.where(ge, -inf, x), jnp.max)),
                    jnp.minimum(c[2], _rows_reduce8(jnp.where(ge, x, inf), jnp.min)))
        cnt, lo, hi = sweep(f, (z8, z8 - inf, z8 + inf))
        return fold(cnt, jnp.sum), fold(lo, jnp.max), fold(hi, jnp.min)

    thr = _search_threshold(count_ge, bracket, scan, fold(amin8, jnp.min), fold(amax8, jnp.max),
                            n0, kf)

    cgt = fold(sweep(lambda x, c: c + _rows_reduce8(jnp.where(x > thr, 1.0, 0.0), jnp.sum), z8),
               jnp.sum)
    need = kf - cgt

    tri = tri_ref[...]

    def p3(u, run):
        for part in range(SUPER // RANK_BLOCK):
            rs = pl.ds(pl.multiple_of(u * SUPER + part * RANK_BLOCK, RANK_BLOCK), RANK_BLOCK)
            x = sc_ref[rs, :]
            eq = x == thr
            rank = jnp.dot(tri, jnp.where(eq, 1.0, 0.0).astype(BF16),
                           preferred_element_type=F32) + run
            keep_tie = jnp.where(rank <= need, 0.0, MASK_BIAS)
            bias = jnp.where(x > thr, 0.0, jnp.where(eq, keep_tie, MASK_BIAS))
            bias_ref[rs, :] = bias.astype(BF16)
            run = rank[RANK_BLOCK - 1:RANK_BLOCK, :]
        return run

    lax.fori_loop(0, nsup, p3, jnp.zeros((1, LANES), F32))

    eye = jnp.where(lax.broadcasted_iota(jnp.int32, (KB, KB), 0)
                    == lax.broadcasted_iota(jnp.int32, (KB, KB), 1), 1.0, 0.0).astype(BF16)
    eye4 = jnp.concatenate([eye] * HEADS_PER_KV, axis=0)
    ags = []
    for g in range(N_KV_HEADS):
        qg = jnp.concatenate(
            [q_ref[:, (g * HEADS_PER_KV + r) * HEAD_DIM:(g * HEADS_PER_KV + r + 1) * HEAD_DIM]
             for r in range(HEADS_PER_KV)], axis=0)
        ags.append(jnp.concatenate([qg, eye4], axis=1))
    m_ref[...] = jnp.full(m_ref.shape, -inf, F32)
    acc_ref[...] = jnp.zeros(acc_ref.shape, F32)
    nrow = HEADS_PER_KV * KB
    part = nrow // ATTN_ROW_PARTS

    def qk(u):
        bias_t = bias_ref[rows(u), :]
        out = []
        for g in range(N_KV_HEADS):
            bmat = jnp.concatenate([kb_ref[rows(u), g * HEAD_DIM:(g + 1) * HEAD_DIM], bias_t],
                                   axis=1)
            out.append(lax.dot_general(ags[g], bmat, _NT,
                                       preferred_element_type=F32).astype(BF16))
        return tuple(out)

    def softmax_pv(u, ts):
        for g in range(N_KV_HEADS):
            vx = vext_ref[g, rows(u), :]
            for hp in range(ATTN_ROW_PARTS):
                rs = slice(hp * part, (hp + 1) * part)
                t = ts[g][rs]
                m_old = m_ref[g, rs]
                m_new = jnp.maximum(m_old, jnp.max(t, axis=1, keepdims=True).astype(F32))
                p = jnp.exp2(t - m_new.astype(BF16))
                acc_ref[g, rs] = (jnp.exp2(m_old - m_new) * acc_ref[g, rs]
                                  + jnp.dot(p, vx, preferred_element_type=F32))
                m_ref[g, rs] = m_new

    def pa(u, ts):
        nxt = qk(u + 1)
        softmax_pv(u, ts)
        return nxt

    softmax_pv(nsup - 1, lax.fori_loop(0, nsup - 1, pa, qk(0)))
    for g in range(N_KV_HEADS):
        acc = acc_ref[g]
        o = acc[:, :HEAD_DIM] / acc[:, HEAD_DIM:]
        for r in range(HEADS_PER_KV):
            h = g * HEADS_PER_KV + r
            o_ref[:, h * HEAD_DIM:(h + 1) * HEAD_DIM] = o[r * KB:(r + 1) * KB].astype(BF16)


def _attn_prompt(q, kb, vb, qi, kib, wi_t):
    B, S, _ = q.shape
    n_sel = min(TOPK_MAX, S // 4)
    qblk = lambda w: pl.BlockSpec((None, KB, w), lambda b, i: (b, i, 0))
    full = lambda w: pl.BlockSpec((None, S, w), lambda b, i: (b, 0, 0))
    nrow = HEADS_PER_KV * KB
    return pl.pallas_call(
        functools.partial(_attn_prompt_kernel, n_sel=n_sel),
        grid=(B, S // KB),
        in_specs=[qblk(Q_W), full(KV_W), full(KV_W), qblk(QI_W), full(IDX_DIM),
                  pl.BlockSpec((None, N_IDX_HEADS, KB), lambda b, i: (b, 0, i))],
        out_specs=qblk(Q_W),
        out_shape=jax.ShapeDtypeStruct((B, S, Q_W), BF16),
        scratch_shapes=[pltpu.VMEM((S, KB), F32), pltpu.VMEM((S, KB), BF16),
                        pltpu.VMEM((N_KV_HEADS, S, 2 * HEAD_DIM), BF16),
                        pltpu.VMEM((RANK_BLOCK, RANK_BLOCK), BF16),
                        pltpu.VMEM((N_KV_HEADS, nrow, 1), F32),
                        pltpu.VMEM((N_KV_HEADS, nrow, 2 * HEAD_DIM), F32)],
        compiler_params=_cparams(2),
        name="attn_prompt",
    )(q, kb, vb, qi, kib, wi_t)


def _idx_scores(qi, wcol, keys_t):
    s = jnp.dot(qi, keys_t, preferred_element_type=F32)
    r = jnp.maximum(s, 0.0) * wcol
    out = r[0:SUBLANES]
    for h in range(1, N_IDX_HEADS):
        out = out + r[h * SUBLANES:(h + 1) * SUBLANES]
    return out


def _score_sample_kernel(pt_ref, qi_ref, w_ref, *refs, past):
    ki_refs = refs[:IDX_PAGES_PER_STEP]
    kinew_ref, sc_ref = refs[IDX_PAGES_PER_STEP:]
    j = pl.program_id(1)
    qi = qi_ref[...]
    wcol = w_ref[:, 0:1]
    width = IDX_PAGES_PER_STEP * PAGE_SIZE
    keys_t = jnp.concatenate([r[...] for r in ki_refs], axis=1).astype(BF16)
    nq = sc_ref.shape[0]
    sc_ref[:, pl.ds(pl.multiple_of(j * width, width), width)] = _idx_scores(qi, wcol, keys_t)[:nq]

    @pl.when(j == pl.num_programs(1) - 1)
    def _():
        inf = jnp.float32(jnp.inf)
        row = lax.broadcasted_iota(jnp.int32, (SUBLANES, PAGE_SIZE), 0)
        lane = lax.broadcasted_iota(jnp.int32, (SUBLANES, PAGE_SIZE), 1)
        snew = _idx_scores(qi, wcol, kinew_ref[...])
        sc_ref[:, past:past + PAGE_SIZE] = jnp.where(lane <= row, snew, -inf)[:nq]
        sc_ref[:, past + PAGE_SIZE:] = jnp.full((nq, SUPER - PAGE_SIZE), -inf, F32)


def _score_sample(page_table, qi_pad, wcol, cache_ki_t, kinew_t, nq):
    B, n_pages = page_table.shape
    past = n_pages * PAGE_SIZE
    steps = n_pages // IDX_PAGES_PER_STEP
    page = lambda p: pl.BlockSpec((None, IDX_DIM, PAGE_SIZE),
                                  lambda b, j, pt: (pt[b, j * IDX_PAGES_PER_STEP + p], 0, 0))
    per_b = lambda r, w: pl.BlockSpec((None, r, w), lambda b, j, pt: (b, 0, 0))
    return pl.pallas_call(
        functools.partial(_score_sample_kernel, past=past),
        grid_spec=pltpu.PrefetchScalarGridSpec(
            num_scalar_prefetch=1,
            grid=(B, steps),
            in_specs=[per_b(N_IDX_HEADS * SUBLANES, IDX_DIM), per_b(N_IDX_HEADS * SUBLANES, LANES)]
                     + [page(p) for p in range(IDX_PAGES_PER_STEP)]
                     + [per_b(IDX_DIM, PAGE_SIZE)],
            out_specs=per_b(nq, past + SUPER),
        ),
        out_shape=jax.ShapeDtypeStruct((B, nq, past + SUPER), F32),
        compiler_params=_cparams(2),
        name="score_sample",
    )(page_table, qi_pad, wcol, *([cache_ki_t] * IDX_PAGES_PER_STEP), kinew_t)


def _lanes_reduce(x, op):
    out = x[:, :LANES]
    for t in range(1, x.shape[1] // LANES):
        out = op(out, x[:, t * LANES:(t + 1) * LANES])
    return out


def _select_sample_kernel(sc_ref, sel_ref, tri_ref, spread_ref, *, n_sel):
    R, L = sc_ref.shape
    nsup = L // SUPER
    kf = float(n_sel)
    inf = jnp.float32(jnp.inf)
    r = lax.broadcasted_iota(jnp.int32, (SUPER, SUPER), 0)
    c = lax.broadcasted_iota(jnp.int32, (SUPER, SUPER), 1)
    tri_ref[...] = jnp.where(r <= c, 1.0, 0.0).astype(BF16)

    def cols(u):
        return pl.ds(pl.multiple_of(u * SUPER, SUPER), SUPER)

    def sweep(fn, init):
        return lax.fori_loop(0, nsup, lambda u, cr: fn(sc_ref[:, cols(u)], cr), init)

    zr = jnp.zeros((R, LANES), F32)
    rsum = lambda a: jnp.sum(a, axis=1, keepdims=True)
    rmax = lambda a: jnp.max(a, axis=1, keepdims=True)
    rmin = lambda a: jnp.min(a, axis=1, keepdims=True)
    one = lambda m: jnp.where(m, 1.0, 0.0)

    def f0(x, cr):
        fin = x > -inf
        return (jnp.minimum(cr[0], _lanes_reduce(jnp.where(fin, x, inf), jnp.minimum)),
                jnp.maximum(cr[1], _lanes_reduce(x, jnp.maximum)),
                cr[2] + _lanes_reduce(one(fin), jnp.add))
    lo8, hi8, n8 = sweep(f0, (zr + inf, zr - inf, zr))

    def count_ge(p):
        return rsum(sweep(lambda x, cr: cr + _lanes_reduce(one(x >= p), jnp.add), zr))

    def bracket(lo, hi):
        def f(x, cr):
            return (jnp.minimum(cr[0], _lanes_reduce(jnp.where(x >= lo, x, inf), jnp.minimum)),
                    jnp.maximum(cr[1], _lanes_reduce(jnp.where(x <= hi, x, -inf), jnp.maximum)))
        a8, b8 = sweep(f, (zr + inf, zr - inf))
        return rmin(a8), rmax(b8)

    def scan(p):
        def f(x, cr):
            ge = x >= p
            return (cr[0] + _lanes_reduce(one(ge), jnp.add),
                    jnp.maximum(cr[1], _lanes_reduce(jnp.where(ge, -inf, x), jnp.maximum)),
                    jnp.minimum(cr[2], _lanes_reduce(jnp.where(ge, x, inf), jnp.minimum)))
        cnt, lo, hi = sweep(f, (zr, zr - inf, zr + inf))
        return rsum(cnt), rmax(lo), rmin(hi)

    thr = _search_threshold(count_ge, bracket, scan, rmin(lo8), rmax(hi8), rsum(n8), kf)
    need = kf - rsum(sweep(lambda x, cr: cr + _lanes_reduce(one(x > thr), jnp.add), zr))

    wide = N_KV_HEADS * SUPER
    spread_ref[...] = jnp.where(
        lax.broadcasted_iota(jnp.int32, (SUPER, wide), 1) // N_KV_HEADS
        == lax.broadcasted_iota(jnp.int32, (SUPER, wide), 0), 1.0, 0.0).astype(BF16)

    def p3(u, run):
        x = sc_ref[:, cols(u)]
        eq = x == thr
        rank = jnp.dot(one(eq).astype(BF16), tri_ref[...], preferred_element_type=F32) + run
        sel = jnp.where(x > thr, 1.0, jnp.where(eq, one(rank <= need), 0.0))
        sel_ref[:, pl.ds(pl.multiple_of(u * wide, wide), wide)] = jnp.dot(
            sel.astype(BF16), spread_ref[...], preferred_element_type=F32)
        return rank[:, SUPER - 1:SUPER]

    lax.fori_loop(0, nsup, p3, jnp.zeros((R, 1), F32))


def _select_sample(sc, n_sel):
    R, L = sc.shape
    return pl.pallas_call(
        functools.partial(_select_sample_kernel, n_sel=n_sel),
        grid=(R // SEL_ROWS,),
        in_specs=[pl.BlockSpec((SEL_ROWS, L), lambda i: (i, 0))],
        out_specs=pl.BlockSpec((SEL_ROWS, N_KV_HEADS * L), lambda i: (i, 0)),
        out_shape=jax.ShapeDtypeStruct((R, N_KV_HEADS * L), F32),
        scratch_shapes=[pltpu.VMEM((SUPER, SUPER), BF16),
                        pltpu.VMEM((SUPER, N_KV_HEADS * SUPER), BF16)],
        compiler_params=_cparams(1),
        name="select_sample",
    )(sc)


def _attn_sample_kernel(pt_ref, q_ref, sel_ref, *refs, past):
    k_refs = refs[:PAGES_PER_STEP]
    v_refs = refs[PAGES_PER_STEP:2 * PAGES_PER_STEP]
    knew_ref, vnew_ref, o_ref, m_ref, acc_ref = refs[2 * PAGES_PER_STEP:]
    j = pl.program_id(1)
    nq = sel_ref.shape[0]
    nrow = nq * N_HEADS
    pw = N_KV_HEADS * PAGE_SIZE
    per_chain = PAGES_PER_STEP // SAMPLE_CHAINS

    @pl.when(j == 0)
    def _():
        m_ref[...] = jnp.full(m_ref.shape, -jnp.inf, F32)
        acc_ref[...] = jnp.zeros(acc_ref.shape, F32)

    q = q_ref[...]

    def update(ch, kk, vv, col):
        n = kk.shape[0]
        s = lax.dot_general(q, kk, _NT, preferred_element_type=F32)
        sel = sel_ref[:, pl.ds(col, n)]
        sel = jnp.concatenate([jnp.broadcast_to(sel[t:t + 1], (N_HEADS, n)) for t in range(nq)],
                              axis=0)
        own = (lax.broadcasted_iota(jnp.int32, (nrow, n), 0) % N_HEADS // HEADS_PER_KV
               == lax.broadcasted_iota(jnp.int32, (nrow, n), 1) % N_KV_HEADS)
        s = jnp.where(own, jnp.where(sel > 0.5, s, MASK_BIAS), MASK_BIAS)
        m_old = m_ref[ch]
        m_new = jnp.maximum(m_old, jnp.max(s, axis=1, keepdims=True))
        pb = jnp.exp2(s - m_new).astype(BF16)
        vext = jnp.concatenate([vv, jnp.ones((n, HEAD_DIM), BF16)], axis=1)
        acc_ref[ch] = (jnp.exp2(m_old - m_new) * acc_ref[ch]
                       + jnp.dot(pb, vext, preferred_element_type=F32))
        m_ref[ch] = m_new

    for ch in range(SAMPLE_CHAINS):
        pages = range(ch * per_chain, (ch + 1) * per_chain)
        kk = jnp.concatenate([k_refs[p][...].astype(BF16) for p in pages], axis=0)
        vv = jnp.concatenate([v_refs[p][...].astype(BF16) for p in pages], axis=0)
        col = pl.multiple_of((j * PAGES_PER_STEP + ch * per_chain) * pw, per_chain * pw)
        update(ch, kk, vv, col)

    @pl.when(j == pl.num_programs(1) - 1)
    def _():
        update(0, knew_ref[...], vnew_ref[...], N_KV_HEADS * past)
        m = m_ref[0]
        for ch in range(1, SAMPLE_CHAINS):
            m = jnp.maximum(m, m_ref[ch])
        acc = jnp.exp2(m_ref[0] - m) * acc_ref[0]
        for ch in range(1, SAMPLE_CHAINS):
            acc = acc + jnp.exp2(m_ref[ch] - m) * acc_ref[ch]
        o_ref[...] = acc[:, :HEAD_DIM] / acc[:, HEAD_DIM:]


def _attn_sample(page_table, q_all, sel, cache_k, cache_v, knew, vnew):
    B, n_pages = page_table.shape
    past = n_pages * PAGE_SIZE
    steps = n_pages // PAGES_PER_STEP
    nq = sel.shape[1]
    nrow = nq * N_HEADS
    pw = N_KV_HEADS * PAGE_SIZE
    page = lambda p: pl.BlockSpec((pw, HEAD_DIM), lambda b, j, pt: (pt[b, j * PAGES_PER_STEP + p], 0))
    per_b = lambda r, w: pl.BlockSpec((None, r, w), lambda b, j, pt: (b, 0, 0))
    return pl.pallas_call(
        functools.partial(_attn_sample_kernel, past=past),
        grid_spec=pltpu.PrefetchScalarGridSpec(
            num_scalar_prefetch=1,
            grid=(B, steps),
            in_specs=[per_b(nrow, HEAD_DIM), per_b(nq, sel.shape[2])]
                     + [page(p) for p in range(PAGES_PER_STEP)] * 2
                     + [per_b(pw, HEAD_DIM), per_b(pw, HEAD_DIM)],
            out_specs=per_b(nrow, HEAD_DIM),
            scratch_shapes=[pltpu.VMEM((SAMPLE_CHAINS, nrow, 1), F32),
                            pltpu.VMEM((SAMPLE_CHAINS, nrow, 2 * HEAD_DIM), F32)],
        ),
        out_shape=jax.ShapeDtypeStruct((B, nrow, HEAD_DIM), F32),
        compiler_params=_cparams(2),
        name="attn_sample",
    )(page_table, q_all, sel, *([cache_k] * PAGES_PER_STEP), *([cache_v] * PAGES_PER_STEP), knew, vnew)


def _ffn_kernel(*refs, sample, final):
    it = iter(refs)
    h_ref, o_ref, wout_ref, g_ref, wup_ref, cw_ref, cb_ref, wdown_ref = (next(it) for _ in range(8))
    gfin_ref = next(it) if final else None
    st1_ref, st2_ref = (next(it), next(it)) if sample else (None, None)
    hout_ref, cst_ref = next(it), next(it)
    act_ref = next(it)
    carry_ref = None if sample else next(it)

    tm = h_ref.shape[0]
    h1 = h_ref[...] + jnp.dot(o_ref[...], wout_ref[...], preferred_element_type=F32)
    xn = _rmsnorm(h1, g_ref[...]).astype(BF16)

    if sample:
        t4 = lax.broadcasted_iota(jnp.int32, (tm, FF_CHUNK), 0) % 4
    else:
        @pl.when(pl.program_id(1) == 0)
        def _():
            carry_ref[...] = jnp.zeros(carry_ref.shape, F32)
        row8 = lax.broadcasted_iota(jnp.int32, (SUBLANES, FF_CHUNK), 0)

    def conv(cols):
        u = jnp.dot(xn, wup_ref[:, cols], preferred_element_type=F32)
        r1 = pltpu.roll(u, 1, 0)
        r2 = pltpu.roll(u, 2, 0)
        if sample:
            s1 = jnp.where(t4 == 0, st1_ref[:, cols], r1)
            s2 = jnp.where(t4 == 0, st2_ref[:, cols], jnp.where(t4 == 1, st1_ref[:, cols], r2))
            cst_ref[:, cols] = u
        else:
            prev = carry_ref[:, cols]
            top1 = jnp.where(row8 < 1, pltpu.roll(prev, 1, 0), r1[:SUBLANES])
            top2 = jnp.where(row8 < 2, pltpu.roll(prev, 2, 0), r2[:SUBLANES])
            s1 = jnp.concatenate([top1, r1[SUBLANES:]], axis=0)
            s2 = jnp.concatenate([top2, r2[SUBLANES:]], axis=0)
            carry_ref[:, cols] = u[tm - SUBLANES:]
            cst_ref[:, cols] = u[tm - SUBLANES:]
        cw = cw_ref[:, cols]
        return cb_ref[:, cols] + cw[0:1] * s2 + cw[1:2] * s1 + cw[2:3] * u

    for c in range(N_FF_CHUNKS):
        gate = conv(slice(c * FF_CHUNK, (c + 1) * FF_CHUNK))
        val = conv(slice(D_FF + c * FF_CHUNK, D_FF + (c + 1) * FF_CHUNK))
        act_ref[:, c * FF_CHUNK:(c + 1) * FF_CHUNK] = (
            gate * (1.0 / (1.0 + jnp.exp(-gate))) * val).astype(BF16)

    out = h1 + jnp.dot(act_ref[...], wdown_ref[...], preferred_element_type=F32)
    if final:
        out = _rmsnorm(out, gfin_ref[...])
    hout_ref[...] = out


def _ffn(h, o, wout, g, wup, cw, cb, wdown, gfin, state, tm):
    B, T, _ = h.shape
    sample = state is not None
    final = gfin is not None
    row = lambda b, t: (b, t, 0)
    const = lambda b, t: (0, 0)
    w = lambda a: pl.BlockSpec(a.shape, const, pipeline_mode=pl.Buffered(1))
    operands = [h, o, wout, g, wup, cw, cb, wdown]
    in_specs = [pl.BlockSpec((None, tm, D_MODEL), row), pl.BlockSpec((None, tm, D_MODEL), row),
                w(wout), w(g), w(wup), w(cw), w(cb), w(wdown)]
    if final:
        operands.append(gfin)
        in_specs.append(w(gfin))
    scratch = [pltpu.VMEM((tm, D_FF), BF16)]
    if sample:
        operands += [state[0], state[1]]
        in_specs += [w(state[0]), w(state[1])]
        cst_shape = jax.ShapeDtypeStruct((B, T, 2 * D_FF), F32)
        cst_spec = pl.BlockSpec((None, tm, 2 * D_FF), row)
    else:
        cst_shape = jax.ShapeDtypeStruct((B, SUBLANES, 2 * D_FF), F32)
        cst_spec = pl.BlockSpec((None, SUBLANES, 2 * D_FF), lambda b, t: (b, 0, 0))
        scratch.append(pltpu.VMEM((SUBLANES, 2 * D_FF), F32))
    return pl.pallas_call(
        functools.partial(_ffn_kernel, sample=sample, final=final),
        grid=(B, T // tm),
        in_specs=in_specs,
        out_specs=[pl.BlockSpec((None, tm, D_MODEL), row), cst_spec],
        out_shape=[jax.ShapeDtypeStruct((B, T, D_MODEL), F32), cst_shape],
        scratch_shapes=scratch,
        compiler_params=_cparams(2),
        name="ffn_sample" if sample else "ffn_prompt",
    )(*operands)


def _sgu_kernel(h_ref, g_ref, win_ref, lng_ref, lnb_ref, ws_ref, bias_ref, *out_refs, emit_v):
    y_ref = out_refs[0]
    tm = h_ref.shape[0]
    xn = _rmsnorm(h_ref[...], g_ref[...]).astype(BF16)
    uv = jnp.dot(xn, win_ref[...], preferred_element_type=F32)
    u = uv[:, :D_MODEL]
    v = uv[:, D_MODEL:]
    xc = v - jnp.mean(v, axis=-1, keepdims=True)
    vn = xc * lax.rsqrt(jnp.mean(xc * xc, axis=-1, keepdims=True) + LN_EPS) * lng_ref[...] + lnb_ref[...]
    if emit_v:
        out_refs[1][...] = vn
    vnb = vn.astype(BF16)
    tri = (lax.broadcasted_iota(jnp.int32, (CHUNK, CHUNK), 0)
           >= lax.broadcasted_iota(jnp.int32, (CHUNK, CHUNK), 1))
    for g in range(SGU_GROUPS):
        wg = jnp.where(tri, ws_ref[g], 0.0).astype(BF16)
        cols = slice(g * SGU_GROUP_DIM, (g + 1) * SGU_GROUP_DIM)
        for c in range(tm // CHUNK):
            rows = slice(c * CHUNK, (c + 1) * CHUNK)
            s = jnp.dot(wg, vnb[rows, cols], preferred_element_type=F32) + bias_ref[:, cols]
            y_ref[rows, cols] = (u[rows, cols] * s).astype(BF16)


def _sgu_bias(bs):
    return jnp.repeat(bs.T, SGU_GROUP_DIM, axis=1)


def _sgu_sample_weights(ws, bs, B, T):
    corner = ws[:, :T, :T]
    ws_s = jnp.einsum('bc,gts->gbtcs', jnp.eye(B, dtype=ws.dtype), corner)
    return ws_s.reshape(SGU_GROUPS, B * T, B * T), jnp.tile(_sgu_bias(bs)[:T], (B, 1))


def _sgu(h, g, win, lng, lnb, ws, bias, tm, emit_v):
    B, T, _ = h.shape
    row = lambda b, t: (b, t, 0)
    w = lambda a: pl.BlockSpec(a.shape, lambda b, t: (0,) * a.ndim)
    out_shape = [jax.ShapeDtypeStruct((B, T, D_MODEL), BF16)]
    out_specs = [pl.BlockSpec((None, tm, D_MODEL), row)]
    if emit_v:
        out_shape.append(jax.ShapeDtypeStruct((B, T, D_MODEL), F32))
        out_specs.append(pl.BlockSpec((None, tm, D_MODEL), row))
    return pl.pallas_call(
        functools.partial(_sgu_kernel, emit_v=emit_v),
        grid=(B, T // tm),
        in_specs=[pl.BlockSpec((None, tm, D_MODEL), row), w(g), w(win), w(lng), w(lnb), w(ws), w(bias)],
        out_specs=out_specs,
        out_shape=out_shape,
        compiler_params=_cparams(2),
        name="sgu",
    )(h, g, win, lng, lnb, ws, bias)


def _pad_axis(a, axis, size):
    pad = [(0, 0)] * a.ndim
    pad[axis] = (0, size - a.shape[axis])
    return jnp.pad(a, pad)


def _sample_attention(qs, ksb, vsb, qis, kws, kisb, page_table, cache_k, cache_v, cache_ki, Bs, Ts):
    Ns = Bs * Ts
    n_pool = cache_k.shape[1]
    past = page_table.shape[1] * PAGE_SIZE
    qi_pad = _pad_axis(qis.reshape(Bs, Ts, N_IDX_HEADS, IDX_DIM).transpose(0, 2, 1, 3), 2, SUBLANES)
    qi_pad = qi_pad.reshape(Bs, N_IDX_HEADS * SUBLANES, IDX_DIM)
    wcol = kws[0, :, IDX_DIM:IDX_DIM + N_IDX_HEADS].reshape(Bs, Ts, N_IDX_HEADS).transpose(0, 2, 1)
    wcol = _pad_axis(wcol, 2, SUBLANES).reshape(Bs, N_IDX_HEADS * SUBLANES, 1)
    wcol = jnp.broadcast_to(wcol, (Bs, N_IDX_HEADS * SUBLANES, LANES))
    kinew_t = jnp.swapaxes(_pad_axis(kisb.reshape(Bs, Ts, IDX_DIM), 1, PAGE_SIZE), 1, 2)
    cache_ki_t = jnp.swapaxes(cache_ki, 2, 3).reshape(n_pool, IDX_DIM, PAGE_SIZE)
    sc = _score_sample(page_table, qi_pad, wcol, cache_ki_t, kinew_t, Ts)
    n_sel = min(TOPK_MAX, (past + Ts) // 4)
    sel = _select_sample(sc.reshape(Ns, past + SUPER), n_sel)
    sel = sel.reshape(Bs, Ts, N_KV_HEADS * (past + SUPER))
    q_all = qs.reshape(Bs, Ts * N_HEADS, HEAD_DIM)
    pw = N_KV_HEADS * PAGE_SIZE
    knew = _pad_axis(ksb.reshape(Bs, Ts * N_KV_HEADS, HEAD_DIM), 1, pw)
    vnew = _pad_axis(vsb.reshape(Bs, Ts * N_KV_HEADS, HEAD_DIM), 1, pw)
    o_s = _attn_sample(page_table, q_all, sel, cache_k.reshape(n_pool * pw, HEAD_DIM),
                       cache_v.reshape(n_pool * pw, HEAD_DIM), knew, vnew)
    return o_s.reshape(1, Ns, Q_W).astype(BF16)


def kernel(x_prompt, x_sample, cache_k, cache_v, cache_idx_k, state_ffn_conv, page_table, attn_norm,
           w_attn_in, w_attn_out, sgu_norm, w_sgu_in, sgu_ln_g, sgu_ln_b, sgu_ws, sgu_bs, w_sgu_out,
           ffn_norm, w_ffn_up, ffn_conv_w, ffn_conv_b, w_ffn_down, final_norm):
    B, S, _ = x_prompt.shape
    Bs, Ts, _ = x_sample.shape
    Ns = Bs * Ts
    n_pages = page_table.shape[1]
    past = n_pages * PAGE_SIZE
    assert w_ffn_up.shape[0] == 2 and w_attn_in.shape[0] == 1 and w_sgu_in.shape[0] == 1
    assert Ns == CHUNK and Ts == 4 and S % SUPER == 0
    assert n_pages % PAGES_PER_STEP == 0 and n_pages % IDX_PAGES_PER_STEP == 0 and Ns % SEL_ROWS == 0
    bf = lambda a: a.astype(BF16)
    r1 = lambda a: a[None]
    tm_p, tm_s = (PROMPT_ROWS if S % PROMPT_ROWS == 0 else SUPER), Ns

    w_in = bf(_pad_axis(w_attn_in[0], 1, ATTN_IN_PAD))
    g_attn = r1(attn_norm[0])
    q, k32, v32, kb, vb, qi, kw, kib = _attn_in(x_prompt, jnp.arange(S), g_attn, w_in, tm_p)
    wi_t = jnp.swapaxes(kw[..., IDX_DIM:IDX_DIM + N_IDX_HEADS], 1, 2)
    o_p = _attn_prompt(q, kb, vb, qi, kib, wi_t)

    xs = x_sample.reshape(1, Ns, D_MODEL)
    pos_s = past + (jnp.arange(Ns) % Ts)
    qs, ks32, vs32, ksb, vsb, qis, kws, kisb = _attn_in(xs, pos_s, g_attn, w_in, tm_s)
    o_s = _sample_attention(qs, ksb, vsb, qis, kws, kisb, page_table,
                            cache_k[0:1], cache_v[0:1], cache_idx_k[0:1], Bs, Ts)

    def ffn(i, h, o, wout, gfin, sample):
        state = None
        if sample:
            st = state_ffn_conv[i]
            state = (jnp.repeat(st[:, 1], Ts, axis=0), jnp.repeat(st[:, 0], Ts, axis=0))
        return _ffn(h, o, bf(wout), r1(ffn_norm[i]), bf(w_ffn_up[i]), ffn_conv_w[i],
                    r1(ffn_conv_b[i]), bf(w_ffn_down[i]), gfin, state, tm_s if sample else tm_p)

    hp, cst_p0 = ffn(0, x_prompt, o_p, w_attn_out[0], None, False)
    hs, up_s0 = ffn(0, xs, o_s, w_attn_out[0], None, True)

    sgu_args = (r1(sgu_norm[0]), bf(w_sgu_in[0]), r1(sgu_ln_g[0]), r1(sgu_ln_b[0]))
    (y_p,) = _sgu(hp, *sgu_args, sgu_ws[0], _sgu_bias(sgu_bs[0]), tm_p, False)
    ws_s, bias_s = _sgu_sample_weights(sgu_ws[0], sgu_bs[0], Bs, Ts)
    y_s, vn_s = _sgu(hs, *sgu_args, ws_s, bias_s, tm_s, True)

    gfin = r1(final_norm)
    yp, cst_p1 = ffn(1, hp, y_p, w_sgu_out[0], gfin, False)
    ys, up_s1 = ffn(1, hs, y_s, w_sgu_out[0], gfin, True)

    tail = lambda c: c[:, SUBLANES - (CONV_W - 1):]
    stail = lambda u: u.reshape(Bs, Ts, 2 * D_FF)[:, Ts - (CONV_W - 1):]
    return (yp, ys.reshape(Bs, Ts, D_MODEL),
            k32.reshape(1, B, S, N_KV_HEADS, HEAD_DIM), v32.reshape(1, B, S, N_KV_HEADS, HEAD_DIM),
            kw[None, ..., :IDX_DIM],
            ks32.reshape(1, Bs, Ts, N_KV_HEADS, HEAD_DIM), vs32.reshape(1, Bs, Ts, N_KV_HEADS, HEAD_DIM),
            kws[..., :IDX_DIM].reshape(1, Bs, Ts, IDX_DIM),
            vn_s.reshape(1, Bs, Ts, D_MODEL),
            jnp.stack([tail(cst_p0), tail(cst_p1)]),
            jnp.stack([stail(up_s0), stail(up_s1)]))
```

```python
import functools

import numpy as np
import jax
import jax.numpy as jnp
from jax import lax
from jax.experimental import pallas as pl
from jax.experimental.pallas import tpu as pltpu

F32 = jnp.float32
BF16 = jnp.bfloat16

D_MODEL = 1024
N_HEADS = 8
HEAD_DIM = 128
N_KV_HEADS = 2
HEADS_PER_KV = N_HEADS // N_KV_HEADS
ROT_DIM = 32
ROPE_THETA = 500000.0
N_IDX_HEADS = 4
IDX_DIM = 64
IDX_ROT_DIM = 16
TOPK_MAX = 256
PAGE_SIZE = 128
Q_W = N_HEADS * HEAD_DIM
KV_W = N_KV_HEADS * HEAD_DIM
QI_W = N_IDX_HEADS * IDX_DIM
ATTN_IN = Q_W + 2 * KV_W + QI_W + IDX_DIM + N_IDX_HEADS
ATTN_IN_PAD = 1920
KW_COL = Q_W + 2 * KV_W + QI_W
CHUNK = 128
SGU_GROUPS = 4
SGU_GROUP_DIM = D_MODEL // SGU_GROUPS
D_FF = 2816
CONV_W = 3
NORM_EPS = 1e-6
LN_EPS = 1e-5
LOG2E = 1.4426950408889634
ATTN_SCALE_LOG2 = HEAD_DIM ** -0.5 * LOG2E
W_IDX_SCALE = (N_IDX_HEADS ** -0.5) * (IDX_DIM ** -0.5)

LANES = 128
SUBLANES = 8
KB = 128
SUPER = 512
ATTN_ROW_PARTS = 4
RANK_BLOCK = 256
FF_CHUNK = 256
PROMPT_ROWS = 1024
N_FF_CHUNKS = D_FF // FF_CHUNK
PAGES_PER_STEP = 16
IDX_PAGES_PER_STEP = 32
SAMPLE_CHAINS = 4
SEL_ROWS = 64
N_BISECT = 20
VMEM_LIMIT = 56 * 1024 * 1024
MASK_BIAS = -1e30

_NT = (((1,), (1,)), ((), ()))


def _cparams(n_axes):
    return pltpu.CompilerParams(
        dimension_semantics=("arbitrary",) * n_axes, vmem_limit_bytes=VMEM_LIMIT)


def _rmsnorm(x, g):
    return x * lax.rsqrt(jnp.mean(x * x, axis=-1, keepdims=True) + NORM_EPS) * g


def _rope(u, c, sa, sb, half):
    return u * c + pltpu.roll(u, LANES - half, 1) * sa + pltpu.roll(u, half, 1) * sb


def _attn_in_kernel(x_ref, g_ref, w_ref, ch_ref, sha_ref, shb_ref, ci_ref, sia_ref, sib_ref,
                    ckw_ref, skwa_ref, skwb_ref,
                    q_ref, k_ref, v_ref, kb_ref, vb_ref, qi_ref, kw_ref, kib_ref):
    xn = _rmsnorm(x_ref[...], g_ref[...]).astype(BF16)
    z = jnp.dot(xn, w_ref[...], preferred_element_type=F32)
    ch, sha, shb = ch_ref[...], sha_ref[...], shb_ref[...]
    for h in range(N_HEADS):
        sl = slice(h * HEAD_DIM, (h + 1) * HEAD_DIM)
        q_ref[:, sl] = (_rope(z[:, sl], ch, sha, shb, ROT_DIM // 2) * ATTN_SCALE_LOG2).astype(BF16)
    for h in range(N_KV_HEADS):
        sl = slice(h * HEAD_DIM, (h + 1) * HEAD_DIM)
        kk = _rope(z[:, Q_W + h * HEAD_DIM:Q_W + (h + 1) * HEAD_DIM], ch, sha, shb, ROT_DIM // 2)
        k_ref[pl.ds(h, kk.shape[0], stride=N_KV_HEADS), :] = kk
        kb_ref[:, sl] = kk.astype(BF16)
    v = z[:, Q_W + KV_W:Q_W + 2 * KV_W]
    for h in range(N_KV_HEADS):
        v_ref[pl.ds(h, v.shape[0], stride=N_KV_HEADS), :] = v[:, h * HEAD_DIM:(h + 1) * HEAD_DIM]
    vb_ref[...] = v.astype(BF16)
    ci, sia, sib = ci_ref[...], sia_ref[...], sib_ref[...]
    qi0 = Q_W + 2 * KV_W
    for s in range(QI_W // LANES):
        sl = slice(s * LANES, (s + 1) * LANES)
        qi_ref[:, sl] = _rope(z[:, qi0 + s * LANES:qi0 + (s + 1) * LANES], ci, sia, sib,
                              IDX_ROT_DIM // 2).astype(BF16)
    kw = _rope(z[:, KW_COL:KW_COL + LANES], ckw_ref[...], skwa_ref[...], skwb_ref[...],
               IDX_ROT_DIM // 2)
    kw_ref[...] = kw
    kib_ref[...] = kw[:, :IDX_DIM].astype(BF16)


def _rope_tables(pos, rot, period, tail_scale=None):
    half = rot // 2
    freqs = ROPE_THETA ** (-(jnp.arange(half, dtype=F32) * 2.0) / rot)
    ang = pos.astype(F32)[:, None] * freqs[None, :]
    cos, sin = jnp.cos(ang), jnp.sin(ang)
    lane = np.arange(LANES) % period
    col = lane % half
    c = jnp.where(lane < rot, cos[:, col], 1.0)
    sa = jnp.where(lane < half, -sin[:, col], 0.0)
    sb = jnp.where((lane >= half) & (lane < rot), sin[:, col], 0.0)
    if tail_scale is not None:
        glane = np.arange(LANES)
        c = jnp.where(glane < period, c, 1.0)
        c = jnp.where((glane >= IDX_DIM) & (glane < IDX_DIM + N_IDX_HEADS), tail_scale, c)
        sa = jnp.where(glane < period, sa, 0.0)
        sb = jnp.where(glane < period, sb, 0.0)
    return c.astype(F32), sa.astype(F32), sb.astype(F32)


def _attn_in(x, pos, g, w_pad, tm):
    B, T, _ = x.shape
    ch, sha, shb = _rope_tables(pos, ROT_DIM, HEAD_DIM)
    ci, sia, sib = _rope_tables(pos, IDX_ROT_DIM, IDX_DIM)
    ckw, skwa, skwb = _rope_tables(pos, IDX_ROT_DIM, IDX_DIM, tail_scale=W_IDX_SCALE)
    row = lambda b, t: (b, t, 0)
    tab = lambda b, t: (t, 0)
    const = lambda b, t: (0, 0)
    blk = lambda w: pl.BlockSpec((None, tm, w), row)
    tspec = pl.BlockSpec((tm, LANES), tab)
    kvrows = pl.BlockSpec((None, tm * N_KV_HEADS, HEAD_DIM), row)
    outs = pl.pallas_call(
        _attn_in_kernel,
        grid=(B, T // tm),
        in_specs=[blk(D_MODEL), pl.BlockSpec((1, D_MODEL), const),
                  pl.BlockSpec((D_MODEL, ATTN_IN_PAD), const),
                  tspec, tspec, tspec, tspec, tspec, tspec, tspec, tspec, tspec],
        out_specs=[blk(Q_W), kvrows, kvrows, blk(KV_W), blk(KV_W), blk(QI_W),
                   blk(LANES), blk(IDX_DIM)],
        out_shape=[jax.ShapeDtypeStruct((B, T, Q_W), BF16),
                   jax.ShapeDtypeStruct((B, T * N_KV_HEADS, HEAD_DIM), F32),
                   jax.ShapeDtypeStruct((B, T * N_KV_HEADS, HEAD_DIM), F32),
                   jax.ShapeDtypeStruct((B, T, KV_W), BF16),
                   jax.ShapeDtypeStruct((B, T, KV_W), BF16),
                   jax.ShapeDtypeStruct((B, T, QI_W), BF16),
                   jax.ShapeDtypeStruct((B, T, LANES), F32),
                   jax.ShapeDtypeStruct((B, T, IDX_DIM), BF16)],
        compiler_params=_cparams(2),
        name="attn_in",
    )(x, g, w_pad, ch, sha, shb, ci, sia, sib, ckw, skwa, skwb)
    return outs


def _search_threshold(count_ge, bracket, scan, lo0, hi0, n0, kf):
    hi0 = jnp.where(n0 <= kf, lo0, hi0)

    def coarse(_, st):
        lo, hi = st
        p = 0.5 * lo + 0.5 * hi
        ge = count_ge(p) >= kf
        return jnp.where(ge, p, lo), jnp.where(ge, hi, p)

    lo, hi = lax.fori_loop(0, N_BISECT, coarse, (lo0, hi0))
    a, b = bracket(lo, hi)

    def n_open(a, b):
        return jnp.max(jnp.where(a < b, 1, 0).astype(jnp.int32))

    def body(carry):
        _, a, b = carry
        p = jnp.minimum(jnp.maximum(0.5 * a + 0.5 * b, a), b)
        p = jnp.where(p > a, p, b)
        cnt, lowmax, highmin = scan(p)
        act = a < b
        a2 = jnp.where(act, jnp.where(cnt >= kf, highmin, a), a)
        b2 = jnp.where(act, jnp.where(cnt >= kf, b, lowmax), b)
        return n_open(a2, b2), a2, b2

    out = lax.while_loop(lambda c: c[0] > 0, body, (n_open(a, b), a, b))
    return out[1]


ACC_ROWS = 64


def _rows_reduce8(x, op):
    return op(x.reshape(x.shape[0] // ACC_ROWS, ACC_ROWS, LANES), axis=0)


def _attn_prompt_kernel(q_ref, kb_ref, vb_ref, qi_ref, kib_ref, wi_ref, o_ref,
                        sc_ref, bias_ref, vext_ref, tri_ref, m_ref, acc_ref, *, n_sel):
    i = pl.program_id(1)
    nsup = i // (SUPER // KB) + 1
    kf = float(n_sel)
    inf = jnp.float32(jnp.inf)

    @pl.when(i == 0)
    def _():
        for g in range(N_KV_HEADS):
            vext_ref[g, :, :HEAD_DIM] = vb_ref[:, g * HEAD_DIM:(g + 1) * HEAD_DIM]
            vext_ref[g, :, HEAD_DIM:] = jnp.ones((vext_ref.shape[1], HEAD_DIM), BF16)
        r = lax.broadcasted_iota(jnp.int32, (RANK_BLOCK, RANK_BLOCK), 0)
        c = lax.broadcasted_iota(jnp.int32, (RANK_BLOCK, RANK_BLOCK), 1)
        tri_ref[...] = jnp.where(c <= r, 1.0, 0.0).astype(BF16)

    def rows(u):
        return pl.ds(pl.multiple_of(u * SUPER, SUPER), SUPER)

    def sweep(fn, init):
        return lax.fori_loop(0, nsup, lambda u, c: fn(sc_ref[rows(u), :], c), init)

    z8 = jnp.zeros((ACC_ROWS, LANES), F32)
    fold = lambda acc8, op: op(acc8, axis=0, keepdims=True)

    w = wi_ref[...]
    qi = qi_ref[...]
    qi_rows = jnp.concatenate([qi[:, h * IDX_DIM:(h + 1) * IDX_DIM] for h in range(N_IDX_HEADS)],
                              axis=0)
    kmq = (lax.broadcasted_iota(jnp.int32, (SUPER, KB), 0)
           - lax.broadcasted_iota(jnp.int32, (SUPER, KB), 1))

    def p1(u, carry):
        amin, amax = carry
        kic = kib_ref[rows(u), :]
        sh = lax.dot_general(kic, qi_rows, _NT, preferred_element_type=F32)
        s = None
        for h in range(N_IDX_HEADS):
            t = w[h:h + 1, :] * jnp.maximum(sh[:, h * KB:(h + 1) * KB], 0.0)
            s = t if s is None else s + t
        causal = kmq <= i * KB - u * SUPER
        x = jnp.where(causal, s, -inf)
        sc_ref[rows(u), :] = x
        amin = jnp.minimum(amin, _rows_reduce8(jnp.where(causal, s, inf), jnp.min))
        amax = jnp.maximum(amax, _rows_reduce8(x, jnp.max))
        return amin, amax

    amin8, amax8 = lax.fori_loop(0, nsup, p1, (z8 + inf, z8 - inf))
    lane = lax.broadcasted_iota(jnp.int32, (1, LANES), 1)
    n0 = (i * KB + lane + 1).astype(F32)

    def count_ge(p):
        return fold(sweep(lambda x, c: c + _rows_reduce8(jnp.where(x >= p, 1.0, 0.0), jnp.sum), z8),
                    jnp.sum)

    def bracket(lo, hi):
        def f(x, c):
            return (jnp.minimum(c[0], _rows_reduce8(jnp.where(x >= lo, x, inf), jnp.min)),
                    jnp.maximum(c[1], _rows_reduce8(jnp.where(x <= hi, x, -inf), jnp.max)))
        a8, b8 = sweep(f, (z8 + inf, z8 - inf))
        return fold(a8, jnp.min), fold(b8, jnp.max)

    def scan(p):
        def f(x, c):
            ge = x >= p
            return (c[0] + _rows_reduce8(jnp.where(ge, 1.0, 0.0), jnp.sum),
                    jnp.maximum(c[1], _rows_reduce8(jnp.where(ge, -inf, x), jnp.max)),
                    jnp.minimum(c[2], _rows_reduce8(jnp.where(ge, x, inf), jnp.min)))
        cnt, lo, hi = sweep(f, (z8, z8 - inf, z8 + inf))
        return fold(cnt, jnp.sum), fold(lo, jnp.max), fold(hi, jnp.min)

    lo0 = fold(amin8, jnp.min)
    thr = lax.cond((i + 1) * KB <= n_sel, lambda: lo0,
                   lambda: _search_threshold(count_ge, bracket, scan, lo0, fold(amax8, jnp.max),
                                             n0, kf))

    cgt = fold(sweep(lambda x, c: c + _rows_reduce8(jnp.where(x > thr, 1.0, 0.0), jnp.sum), z8),
               jnp.sum)
    need = kf - cgt

    tri = tri_ref[...]

    def p3(u, run):
        for part in range(SUPER // RANK_BLOCK):
            rs = pl.ds(pl.multiple_of(u * SUPER + part * RANK_BLOCK, RANK_BLOCK), RANK_BLOCK)
            x = sc_ref[rs, :]
            eq = x == thr
            rank = jnp.dot(tri, jnp.where(eq, 1.0, 0.0).astype(BF16),
                           preferred_element_type=F32) + run
            keep_tie = jnp.where(rank <= need, 0.0, MASK_BIAS)
            bias = jnp.where(x > thr, 0.0, jnp.where(eq, keep_tie, MASK_BIAS))
            bias_ref[rs, :] = bias.astype(BF16)
            run = rank[RANK_BLOCK - 1:RANK_BLOCK, :]
        return run

    lax.fori_loop(0, nsup, p3, jnp.zeros((1, LANES), F32))

    eye = jnp.where(lax.broadcasted_iota(jnp.int32, (KB, KB), 0)
                    == lax.broadcasted_iota(jnp.int32, (KB, KB), 1), 1.0, 0.0).astype(BF16)
    eye4 = jnp.concatenate([eye] * HEADS_PER_KV, axis=0)
    ags = []
    for g in range(N_KV_HEADS):
        qg = jnp.concatenate(
            [q_ref[:, (g * HEADS_PER_KV + r) * HEAD_DIM:(g * HEADS_PER_KV + r + 1) * HEAD_DIM]
             for r in range(HEADS_PER_KV)], axis=0)
        ags.append(jnp.concatenate([qg, eye4], axis=1))
    m_ref[...] = jnp.full(m_ref.shape, -inf, F32)
    acc_ref[...] = jnp.zeros(acc_ref.shape, F32)
    nrow = HEADS_PER_KV * KB
    part = nrow // ATTN_ROW_PARTS

    def qk(u):
        bias_t = bias_ref[rows(u), :]
        out = []
        for g in range(N_KV_HEADS):
            bmat = jnp.concatenate([kb_ref[rows(u), g * HEAD_DIM:(g + 1) * HEAD_DIM], bias_t],
                                   axis=1)
            out.append(lax.dot_general(ags[g], bmat, _NT,
                                       preferred_element_type=F32).astype(BF16))
        return tuple(out)

    def softmax_pv(u, ts, nkeys=SUPER):
        for g in range(N_KV_HEADS):
            vx = vext_ref[g, pl.ds(pl.multiple_of(u * SUPER, SUPER), nkeys), :]
            for hp in range(ATTN_ROW_PARTS):
                rs = slice(hp * part, (hp + 1) * part)
                t = ts[g][rs, :nkeys]
                m_old = m_ref[g, rs]
                m_new = jnp.maximum(m_old, jnp.max(t, axis=1, keepdims=True).astype(F32))
                p = jnp.exp2(t - m_new.astype(BF16))
                acc_ref[g, rs] = (jnp.exp2(m_old - m_new) * acc_ref[g, rs]
                                  + jnp.dot(p, vx, preferred_element_type=F32))
                m_ref[g, rs] = m_new

    def pa(u, ts):
        nxt = qk(u + 1)
        softmax_pv(u, ts)
        return nxt

    ts_last = lax.fori_loop(0, nsup - 1, pa, qk(0))
    blocks_per_super = SUPER // KB
    short = i % blocks_per_super < blocks_per_super // 2

    @pl.when(short)
    def _():
        softmax_pv(nsup - 1, ts_last, SUPER // 2)

    @pl.when(jnp.logical_not(short))
    def _():
        softmax_pv(nsup - 1, ts_last)

    for g in range(N_KV_HEADS):
        acc = acc_ref[g]
        o = acc[:, :HEAD_DIM] / acc[:, HEAD_DIM:]
        for r in range(HEADS_PER_KV):
            h = g * HEADS_PER_KV + r
            o_ref[:, h * HEAD_DIM:(h + 1) * HEAD_DIM] = o[r * KB:(r + 1) * KB].astype(BF16)


def _attn_prompt(q, kb, vb, qi, kib, wi_t):
    B, S, _ = q.shape
    n_sel = min(TOPK_MAX, S // 4)
    qblk = lambda w: pl.BlockSpec((None, KB, w), lambda b, i: (b, i, 0))
    full = lambda w: pl.BlockSpec((None, S, w), lambda b, i: (b, 0, 0))
    nrow = HEADS_PER_KV * KB
    return pl.pallas_call(
        functools.partial(_attn_prompt_kernel, n_sel=n_sel),
        grid=(B, S // KB),
        in_specs=[qblk(Q_W), full(KV_W), full(KV_W), qblk(QI_W), full(IDX_DIM),
                  pl.BlockSpec((None, N_IDX_HEADS, KB), lambda b, i: (b, 0, i))],
        out_specs=qblk(Q_W),
        out_shape=jax.ShapeDtypeStruct((B, S, Q_W), BF16),
        scratch_shapes=[pltpu.VMEM((S, KB), F32), pltpu.VMEM((S, KB), BF16),
                        pltpu.VMEM((N_KV_HEADS, S, 2 * HEAD_DIM), BF16),
                        pltpu.VMEM((RANK_BLOCK, RANK_BLOCK), BF16),
                        pltpu.VMEM((N_KV_HEADS, nrow, 1), F32),
                        pltpu.VMEM((N_KV_HEADS, nrow, 2 * HEAD_DIM), F32)],
        compiler_params=_cparams(2),
        name="attn_prompt",
    )(q, kb, vb, qi, kib, wi_t)


def _idx_scores(qi, wcol, keys_t):
    s = jnp.dot(qi, keys_t, preferred_element_type=F32)
    r = jnp.maximum(s, 0.0) * wcol
    out = r[0:SUBLANES]
    for h in range(1, N_IDX_HEADS):
        out = out + r[h * SUBLANES:(h + 1) * SUBLANES]
    return out


def _score_sample_kernel(pt_ref, qi_ref, w_ref, *refs, past):
    ki_refs = refs[:IDX_PAGES_PER_STEP]
    kinew_ref, sc_ref = refs[IDX_PAGES_PER_STEP:]
    j = pl.program_id(1)
    qi = qi_ref[...]
    wcol = w_ref[:, 0:1]
    width = IDX_PAGES_PER_STEP * PAGE_SIZE
    keys_t = jnp.concatenate([r[...] for r in ki_refs], axis=1).astype(BF16)
    nq = sc_ref.shape[0]
    sc_ref[:, pl.ds(pl.multiple_of(j * width, width), width)] = _idx_scores(qi, wcol, keys_t)[:nq]

    @pl.when(j == pl.num_programs(1) - 1)
    def _():
        inf = jnp.float32(jnp.inf)
        row = lax.broadcasted_iota(jnp.int32, (SUBLANES, PAGE_SIZE), 0)
        lane = lax.broadcasted_iota(jnp.int32, (SUBLANES, PAGE_SIZE), 1)
        snew = _idx_scores(qi, wcol, kinew_ref[...])
        sc_ref[:, past:past + PAGE_SIZE] = jnp.where(lane <= row, snew, -inf)[:nq]
        sc_ref[:, past + PAGE_SIZE:] = jnp.full((nq, SUPER - PAGE_SIZE), -inf, F32)


def _score_sample(page_table, qi_pad, wcol, cache_ki_t, kinew_t, nq):
    B, n_pages = page_table.shape
    past = n_pages * PAGE_SIZE
    steps = n_pages // IDX_PAGES_PER_STEP
    page = lambda p: pl.BlockSpec((None, IDX_DIM, PAGE_SIZE),
                                  lambda b, j, pt: (pt[b, j * IDX_PAGES_PER_STEP + p], 0, 0))
    per_b = lambda r, w: pl.BlockSpec((None, r, w), lambda b, j, pt: (b, 0, 0))
    return pl.pallas_call(
        functools.partial(_score_sample_kernel, past=past),
        grid_spec=pltpu.PrefetchScalarGridSpec(
            num_scalar_prefetch=1,
            grid=(B, steps),
            in_specs=[per_b(N_IDX_HEADS * SUBLANES, IDX_DIM), per_b(N_IDX_HEADS * SUBLANES, LANES)]
                     + [page(p) for p in range(IDX_PAGES_PER_STEP)]
                     + [per_b(IDX_DIM, PAGE_SIZE)],
            out_specs=per_b(nq, past + SUPER),
        ),
        out_shape=jax.ShapeDtypeStruct((B, nq, past + SUPER), F32),
        compiler_params=_cparams(2),
        name="score_sample",
    )(page_table, qi_pad, wcol, *([cache_ki_t] * IDX_PAGES_PER_STEP), kinew_t)


def _lanes_reduce(x, op):
    out = x[:, :LANES]
    for t in range(1, x.shape[1] // LANES):
        out = op(out, x[:, t * LANES:(t + 1) * LANES])
    return out


def _select_sample_kernel(sc_ref, sel_ref, tri_ref, spread_ref, *, n_sel):
    R, L = sc_ref.shape
    nsup = L // SUPER
    kf = float(n_sel)
    inf = jnp.float32(jnp.inf)
    r = lax.broadcasted_iota(jnp.int32, (SUPER, SUPER), 0)
    c = lax.broadcasted_iota(jnp.int32, (SUPER, SUPER), 1)
    tri_ref[...] = jnp.where(r <= c, 1.0, 0.0).astype(BF16)

    def cols(u):
        return pl.ds(pl.multiple_of(u * SUPER, SUPER), SUPER)

    def sweep(fn, init):
        return lax.fori_loop(0, nsup, lambda u, cr: fn(sc_ref[:, cols(u)], cr), init)

    zr = jnp.zeros((R, LANES), F32)
    rsum = lambda a: jnp.sum(a, axis=1, keepdims=True)
    rmax = lambda a: jnp.max(a, axis=1, keepdims=True)
    rmin = lambda a: jnp.min(a, axis=1, keepdims=True)
    one = lambda m: jnp.where(m, 1.0, 0.0)

    def f0(x, cr):
        fin = x > -inf
        return (jnp.minimum(cr[0], _lanes_reduce(jnp.where(fin, x, inf), jnp.minimum)),
                jnp.maximum(cr[1], _lanes_reduce(x, jnp.maximum)),
                cr[2] + _lanes_reduce(one(fin), jnp.add))
    lo8, hi8, n8 = sweep(f0, (zr + inf, zr - inf, zr))

    def count_ge(p):
        return rsum(sweep(lambda x, cr: cr + _lanes_reduce(one(x >= p), jnp.add), zr))

    def bracket(lo, hi):
        def f(x, cr):
            return (jnp.minimum(cr[0], _lanes_reduce(jnp.where(x >= lo, x, inf), jnp.minimum)),
                    jnp.maximum(cr[1], _lanes_reduce(jnp.where(x <= hi, x, -inf), jnp.maximum)))
        a8, b8 = sweep(f, (zr + inf, zr - inf))
        return rmin(a8), rmax(b8)

    def scan(p):
        def f(x, cr):
            ge = x >= p
            return (cr[0] + _lanes_reduce(one(ge), jnp.add),
                    jnp.maximum(cr[1], _lanes_reduce(jnp.where(ge, -inf, x), jnp.maximum)),
                    jnp.minimum(cr[2], _lanes_reduce(jnp.where(ge, x, inf), jnp.minimum)))
        cnt, lo, hi = sweep(f, (zr, zr - inf, zr + inf))
        return rsum(cnt), rmax(lo), rmin(hi)

    thr = _search_threshold(count_ge, bracket, scan, rmin(lo8), rmax(hi8), rsum(n8), kf)
    need = kf - rsum(sweep(lambda x, cr: cr + _lanes_reduce(one(x > thr), jnp.add), zr))

    wide = N_KV_HEADS * SUPER
    spread_ref[...] = jnp.where(
        lax.broadcasted_iota(jnp.int32, (SUPER, wide), 1) // N_KV_HEADS
        == lax.broadcasted_iota(jnp.int32, (SUPER, wide), 0), 1.0, 0.0).astype(BF16)

    def p3(u, run):
        x = sc_ref[:, cols(u)]
        eq = x == thr
        rank = jnp.dot(one(eq).astype(BF16), tri_ref[...], preferred_element_type=F32) + run
        sel = jnp.where(x > thr, 1.0, jnp.where(eq, one(rank <= need), 0.0))
        sel_ref[:, pl.ds(pl.multiple_of(u * wide, wide), wide)] = jnp.dot(
            sel.astype(BF16), spread_ref[...], preferred_element_type=F32)
        return rank[:, SUPER - 1:SUPER]

    lax.fori_loop(0, nsup, p3, jnp.zeros((R, 1), F32))


def _select_sample(sc, n_sel):
    R, L = sc.shape
    return pl.pallas_call(
        functools.partial(_select_sample_kernel, n_sel=n_sel),
        grid=(R // SEL_ROWS,),
        in_specs=[pl.BlockSpec((SEL_ROWS, L), lambda i: (i, 0))],
        out_specs=pl.BlockSpec((SEL_ROWS, N_KV_HEADS * L), lambda i: (i, 0)),
        out_shape=jax.ShapeDtypeStruct((R, N_KV_HEADS * L), F32),
        scratch_shapes=[pltpu.VMEM((SUPER, SUPER), BF16),
                        pltpu.VMEM((SUPER, N_KV_HEADS * SUPER), BF16)],
        compiler_params=_cparams(1),
        name="select_sample",
    )(sc)


def _attn_sample_kernel(pt_ref, q_ref, sel_ref, *refs, past):
    k_refs = refs[:PAGES_PER_STEP]
    v_refs = refs[PAGES_PER_STEP:2 * PAGES_PER_STEP]
    knew_ref, vnew_ref, o_ref, m_ref, acc_ref = refs[2 * PAGES_PER_STEP:]
    j = pl.program_id(1)
    nq = sel_ref.shape[0]
    nrow = nq * N_HEADS
    pw = N_KV_HEADS * PAGE_SIZE
    per_chain = PAGES_PER_STEP // SAMPLE_CHAINS

    @pl.when(j == 0)
    def _():
        m_ref[...] = jnp.full(m_ref.shape, -jnp.inf, F32)
        acc_ref[...] = jnp.zeros(acc_ref.shape, F32)

    q = q_ref[...]

    def update(ch, kk, vv, col):
        n = kk.shape[0]
        s = lax.dot_general(q, kk, _NT, preferred_element_type=F32)
        sel = sel_ref[:, pl.ds(col, n)]
        sel = jnp.concatenate([jnp.broadcast_to(sel[t:t + 1], (N_HEADS, n)) for t in range(nq)],
                              axis=0)
        own = (lax.broadcasted_iota(jnp.int32, (nrow, n), 0) % N_HEADS // HEADS_PER_KV
               == lax.broadcasted_iota(jnp.int32, (nrow, n), 1) % N_KV_HEADS)
        s = jnp.where(own, jnp.where(sel > 0.5, s, MASK_BIAS), MASK_BIAS)
        m_old = m_ref[ch]
        m_new = jnp.maximum(m_old, jnp.max(s, axis=1, keepdims=True))
        pb = jnp.exp2(s - m_new).astype(BF16)
        vext = jnp.concatenate([vv, jnp.ones((n, HEAD_DIM), BF16)], axis=1)
        acc_ref[ch] = (jnp.exp2(m_old - m_new) * acc_ref[ch]
                       + jnp.dot(pb, vext, preferred_element_type=F32))
        m_ref[ch] = m_new

    for ch in range(SAMPLE_CHAINS):
        pages = range(ch * per_chain, (ch + 1) * per_chain)
        kk = jnp.concatenate([k_refs[p][...].astype(BF16) for p in pages], axis=0)
        vv = jnp.concatenate([v_refs[p][...].astype(BF16) for p in pages], axis=0)
        col = pl.multiple_of((j * PAGES_PER_STEP + ch * per_chain) * pw, per_chain * pw)
        update(ch, kk, vv, col)

    @pl.when(j == pl.num_programs(1) - 1)
    def _():
        update(0, knew_ref[...], vnew_ref[...], N_KV_HEADS * past)
        m = m_ref[0]
        for ch in range(1, SAMPLE_CHAINS):
            m = jnp.maximum(m, m_ref[ch])
        acc = jnp.exp2(m_ref[0] - m) * acc_ref[0]
        for ch in range(1, SAMPLE_CHAINS):
            acc = acc + jnp.exp2(m_ref[ch] - m) * acc_ref[ch]
        o_ref[...] = acc[:, :HEAD_DIM] / acc[:, HEAD_DIM:]


def _attn_sample(page_table, q_all, sel, cache_k, cache_v, knew, vnew):
    B, n_pages = page_table.shape
    past = n_pages * PAGE_SIZE
    steps = n_pages // PAGES_PER_STEP
    nq = sel.shape[1]
    nrow = nq * N_HEADS
    pw = N_KV_HEADS * PAGE_SIZE
    page = lambda p: pl.BlockSpec((pw, HEAD_DIM), lambda b, j, pt: (pt[b, j * PAGES_PER_STEP + p], 0))
    per_b = lambda r, w: pl.BlockSpec((None, r, w), lambda b, j, pt: (b, 0, 0))
    return pl.pallas_call(
        functools.partial(_attn_sample_kernel, past=past),
        grid_spec=pltpu.PrefetchScalarGridSpec(
            num_scalar_prefetch=1,
            grid=(B, steps),
            in_specs=[per_b(nrow, HEAD_DIM), per_b(nq, sel.shape[2])]
                     + [page(p) for p in range(PAGES_PER_STEP)] * 2
                     + [per_b(pw, HEAD_DIM), per_b(pw, HEAD_DIM)],
            out_specs=per_b(nrow, HEAD_DIM),
            scratch_shapes=[pltpu.VMEM((SAMPLE_CHAINS, nrow, 1), F32),
                            pltpu.VMEM((SAMPLE_CHAINS, nrow, 2 * HEAD_DIM), F32)],
        ),
        out_shape=jax.ShapeDtypeStruct((B, nrow, HEAD_DIM), F32),
        compiler_params=_cparams(2),
        name="attn_sample",
    )(page_table, q_all, sel, *([cache_k] * PAGES_PER_STEP), *([cache_v] * PAGES_PER_STEP), knew, vnew)


def _ffn_kernel(*refs, sample, final):
    it = iter(refs)
    h_ref, o_ref, wout_ref, g_ref, wup_ref, cw_ref, cb_ref, wdown_ref = (next(it) for _ in range(8))
    gfin_ref = next(it) if final else None
    st1_ref, st2_ref = (next(it), next(it)) if sample else (None, None)
    hout_ref, cst_ref = next(it), next(it)
    act_ref = next(it)
    carry_ref = None if sample else next(it)

    tm = h_ref.shape[0]
    h1 = h_ref[...] + jnp.dot(o_ref[...], wout_ref[...], preferred_element_type=F32)
    xn = _rmsnorm(h1, g_ref[...]).astype(BF16)

    if sample:
        t4 = lax.broadcasted_iota(jnp.int32, (tm, FF_CHUNK), 0) % 4
    else:
        @pl.when(pl.program_id(1) == 0)
        def _():
            carry_ref[...] = jnp.zeros(carry_ref.shape, F32)
        row8 = lax.broadcasted_iota(jnp.int32, (SUBLANES, FF_CHUNK), 0)

    def conv(cols):
        u = jnp.dot(xn, wup_ref[:, cols], preferred_element_type=F32)
        r1 = pltpu.roll(u, 1, 0)
        r2 = pltpu.roll(u, 2, 0)
        if sample:
            s1 = jnp.where(t4 == 0, st1_ref[:, cols], r1)
            s2 = jnp.where(t4 == 0, st2_ref[:, cols], jnp.where(t4 == 1, st1_ref[:, cols], r2))
            cst_ref[:, cols] = u
        else:
            prev = carry_ref[:, cols]
            top1 = jnp.where(row8 < 1, pltpu.roll(prev, 1, 0), r1[:SUBLANES])
            top2 = jnp.where(row8 < 2, pltpu.roll(prev, 2, 0), r2[:SUBLANES])
            s1 = jnp.concatenate([top1, r1[SUBLANES:]], axis=0)
            s2 = jnp.concatenate([top2, r2[SUBLANES:]], axis=0)
            carry_ref[:, cols] = u[tm - SUBLANES:]
            cst_ref[:, cols] = u[tm - SUBLANES:]
        cw = cw_ref[:, cols]
        return cb_ref[:, cols] + cw[0:1] * s2 + cw[1:2] * s1 + cw[2:3] * u

    for c in range(N_FF_CHUNKS):
        gate = conv(slice(c * FF_CHUNK, (c + 1) * FF_CHUNK))
        val = conv(slice(D_FF + c * FF_CHUNK, D_FF + (c + 1) * FF_CHUNK))
        act_ref[:, c * FF_CHUNK:(c + 1) * FF_CHUNK] = (
            gate * (1.0 / (1.0 + jnp.exp(-gate))) * val).astype(BF16)

    out = h1 + jnp.dot(act_ref[...], wdown_ref[...], preferred_element_type=F32)
    if final:
        out = _rmsnorm(out, gfin_ref[...])
    hout_ref[...] = out


def _ffn(h, o, wout, g, wup, cw, cb, wdown, gfin, state, tm):
    B, T, _ = h.shape
    sample = state is not None
    final = gfin is not None
    row = lambda b, t: (b, t, 0)
    const = lambda b, t: (0, 0)
    w = lambda a: pl.BlockSpec(a.shape, const, pipeline_mode=pl.Buffered(1))
    operands = [h, o, wout, g, wup, cw, cb, wdown]
    in_specs = [pl.BlockSpec((None, tm, D_MODEL), row), pl.BlockSpec((None, tm, D_MODEL), row),
                w(wout), w(g), w(wup), w(cw), w(cb), w(wdown)]
    if final:
        operands.append(gfin)
        in_specs.append(w(gfin))
    scratch = [pltpu.VMEM((tm, D_FF), BF16)]
    if sample:
        operands += [state[0], state[1]]
        in_specs += [w(state[0]), w(state[1])]
        cst_shape = jax.ShapeDtypeStruct((B, T, 2 * D_FF), F32)
        cst_spec = pl.BlockSpec((None, tm, 2 * D_FF), row)
    else:
        cst_shape = jax.ShapeDtypeStruct((B, SUBLANES, 2 * D_FF), F32)
        cst_spec = pl.BlockSpec((None, SUBLANES, 2 * D_FF), lambda b, t: (b, 0, 0))
        scratch.append(pltpu.VMEM((SUBLANES, 2 * D_FF), F32))
    return pl.pallas_call(
        functools.partial(_ffn_kernel, sample=sample, final=final),
        grid=(B, T // tm),
        in_specs=in_specs,
        out_specs=[pl.BlockSpec((None, tm, D_MODEL), row), cst_spec],
        out_shape=[jax.ShapeDtypeStruct((B, T, D_MODEL), F32), cst_shape],
        scratch_shapes=scratch,
        compiler_params=_cparams(2),
        name="ffn_sample" if sample else "ffn_prompt",
    )(*operands)


def _sgu_kernel(h_ref, g_ref, win_ref, lng_ref, lnb_ref, ws_ref, bias_ref, *out_refs, emit_v):
    y_ref = out_refs[0]
    tm = h_ref.shape[0]
    xn = _rmsnorm(h_ref[...], g_ref[...]).astype(BF16)
    uv = jnp.dot(xn, win_ref[...], preferred_element_type=F32)
    u = uv[:, :D_MODEL]
    v = uv[:, D_MODEL:]
    xc = v - jnp.mean(v, axis=-1, keepdims=True)
    vn = xc * lax.rsqrt(jnp.mean(xc * xc, axis=-1, keepdims=True) + LN_EPS) * lng_ref[...] + lnb_ref[...]
    if emit_v:
        out_refs[1][...] = vn
    vnb = vn.astype(BF16)
    tri = (lax.broadcasted_iota(jnp.int32, (CHUNK, CHUNK), 0)
           >= lax.broadcasted_iota(jnp.int32, (CHUNK, CHUNK), 1))
    for g in range(SGU_GROUPS):
        wg = jnp.where(tri, ws_ref[g], 0.0).astype(BF16)
        cols = slice(g * SGU_GROUP_DIM, (g + 1) * SGU_GROUP_DIM)
        for c in range(tm // CHUNK):
            rows = slice(c * CHUNK, (c + 1) * CHUNK)
            s = jnp.dot(wg, vnb[rows, cols], preferred_element_type=F32) + bias_ref[:, cols]
            y_ref[rows, cols] = (u[rows, cols] * s).astype(BF16)


def _sgu_bias(bs):
    return jnp.repeat(bs.T, SGU_GROUP_DIM, axis=1)


def _sgu_sample_weights(ws, bs, B, T):
    corner = ws[:, :T, :T]
    ws_s = jnp.einsum('bc,gts->gbtcs', jnp.eye(B, dtype=ws.dtype), corner)
    return ws_s.reshape(SGU_GROUPS, B * T, B * T), jnp.tile(_sgu_bias(bs)[:T], (B, 1))


def _sgu(h, g, win, lng, lnb, ws, bias, tm, emit_v):
    B, T, _ = h.shape
    row = lambda b, t: (b, t, 0)
    w = lambda a: pl.BlockSpec(a.shape, lambda b, t: (0,) * a.ndim)
    out_shape = [jax.ShapeDtypeStruct((B, T, D_MODEL), BF16)]
    out_specs = [pl.BlockSpec((None, tm, D_MODEL), row)]
    if emit_v:
        out_shape.append(jax.ShapeDtypeStruct((B, T, D_MODEL), F32))
        out_specs.append(pl.BlockSpec((None, tm, D_MODEL), row))
    return pl.pallas_call(
        functools.partial(_sgu_kernel, emit_v=emit_v),
        grid=(B, T // tm),
        in_specs=[pl.BlockSpec((None, tm, D_MODEL), row), w(g), w(win), w(lng), w(lnb), w(ws), w(bias)],
        out_specs=out_specs,
        out_shape=out_shape,
        compiler_params=_cparams(2),
        name="sgu",
    )(h, g, win, lng, lnb, ws, bias)


def _pad_axis(a, axis, size):
    pad = [(0, 0)] * a.ndim
    pad[axis] = (0, size - a.shape[axis])
    return jnp.pad(a, pad)


def _sample_attention(qs, ksb, vsb, qis, kws, kisb, page_table, cache_k, cache_v, cache_ki, Bs, Ts):
    Ns = Bs * Ts
    n_pool = cache_k.shape[1]
    past = page_table.shape[1] * PAGE_SIZE
    qi_pad = _pad_axis(qis.reshape(Bs, Ts, N_IDX_HEADS, IDX_DIM).transpose(0, 2, 1, 3), 2, SUBLANES)
    qi_pad = qi_pad.reshape(Bs, N_IDX_HEADS * SUBLANES, IDX_DIM)
    wcol = kws[0, :, IDX_DIM:IDX_DIM + N_IDX_HEADS].reshape(Bs, Ts, N_IDX_HEADS).transpose(0, 2, 1)
    wcol = _pad_axis(wcol, 2, SUBLANES).reshape(Bs, N_IDX_HEADS * SUBLANES, 1)
    wcol = jnp.broadcast_to(wcol, (Bs, N_IDX_HEADS * SUBLANES, LANES))
    kinew_t = jnp.swapaxes(_pad_axis(kisb.reshape(Bs, Ts, IDX_DIM), 1, PAGE_SIZE), 1, 2)
    cache_ki_t = jnp.swapaxes(cache_ki, 2, 3).reshape(n_pool, IDX_DIM, PAGE_SIZE)
    sc = _score_sample(page_table, qi_pad, wcol, cache_ki_t, kinew_t, Ts)
    n_sel = min(TOPK_MAX, (past + Ts) // 4)
    sel = _select_sample(sc.reshape(Ns, past + SUPER), n_sel)
    sel = sel.reshape(Bs, Ts, N_KV_HEADS * (past + SUPER))
    q_all = qs.reshape(Bs, Ts * N_HEADS, HEAD_DIM)
    pw = N_KV_HEADS * PAGE_SIZE
    knew = _pad_axis(ksb.reshape(Bs, Ts * N_KV_HEADS, HEAD_DIM), 1, pw)
    vnew = _pad_axis(vsb.reshape(Bs, Ts * N_KV_HEADS, HEAD_DIM), 1, pw)
    o_s = _attn_sample(page_table, q_all, sel, cache_k.reshape(n_pool * pw, HEAD_DIM),
                       cache_v.reshape(n_pool * pw, HEAD_DIM), knew, vnew)
    return o_s.reshape(1, Ns, Q_W).astype(BF16)


def kernel(x_prompt, x_sample, cache_k, cache_v, cache_idx_k, state_ffn_conv, page_table, attn_norm,
           w_attn_in, w_attn_out, sgu_norm, w_sgu_in, sgu_ln_g, sgu_ln_b, sgu_ws, sgu_bs, w_sgu_out,
           ffn_norm, w_ffn_up, ffn_conv_w, ffn_conv_b, w_ffn_down, final_norm):
    B, S, _ = x_prompt.shape
    Bs, Ts, _ = x_sample.shape
    Ns = Bs * Ts
    n_pages = page_table.shape[1]
    past = n_pages * PAGE_SIZE
    assert w_ffn_up.shape[0] == 2 and w_attn_in.shape[0] == 1 and w_sgu_in.shape[0] == 1
    assert Ns == CHUNK and Ts == 4 and S % SUPER == 0
    assert n_pages % PAGES_PER_STEP == 0 and n_pages % IDX_PAGES_PER_STEP == 0 and Ns % SEL_ROWS == 0
    bf = lambda a: a.astype(BF16)
    r1 = lambda a: a[None]
    tm_p, tm_s = (PROMPT_ROWS if S % PROMPT_ROWS == 0 else SUPER), Ns

    w_in = bf(_pad_axis(w_attn_in[0], 1, ATTN_IN_PAD))
    g_attn = r1(attn_norm[0])
    q, k32, v32, kb, vb, qi, kw, kib = _attn_in(x_prompt, jnp.arange(S), g_attn, w_in, tm_p)
    wi_t = jnp.swapaxes(kw[..., IDX_DIM:IDX_DIM + N_IDX_HEADS], 1, 2)
    o_p = _attn_prompt(q, kb, vb, qi, kib, wi_t)

    xs = x_sample.reshape(1, Ns, D_MODEL)
    pos_s = past + (jnp.arange(Ns) % Ts)
    qs, ks32, vs32, ksb, vsb, qis, kws, kisb = _attn_in(xs, pos_s, g_attn, w_in, tm_s)
    o_s = _sample_attention(qs, ksb, vsb, qis, kws, kisb, page_table,
                            cache_k[0:1], cache_v[0:1], cache_idx_k[0:1], Bs, Ts)

    def ffn(i, h, o, wout, gfin, sample):
        state = None
        if sample:
            st = state_ffn_conv[i]
            state = (jnp.repeat(st[:, 1], Ts, axis=0), jnp.repeat(st[:, 0], Ts, axis=0))
        return _ffn(h, o, bf(wout), r1(ffn_norm[i]), bf(w_ffn_up[i]), ffn_conv_w[i],
                    r1(ffn_conv_b[i]), bf(w_ffn_down[i]), gfin, state, tm_s if sample else tm_p)

    hp, cst_p0 = ffn(0, x_prompt, o_p, w_attn_out[0], None, False)
    hs, up_s0 = ffn(0, xs, o_s, w_attn_out[0], None, True)

    sgu_args = (r1(sgu_norm[0]), bf(w_sgu_in[0]), r1(sgu_ln_g[0]), r1(sgu_ln_b[0]))
    (y_p,) = _sgu(hp, *sgu_args, sgu_ws[0], _sgu_bias(sgu_bs[0]), tm_p, False)
    ws_s, bias_s = _sgu_sample_weights(sgu_ws[0], sgu_bs[0], Bs, Ts)
    y_s, vn_s = _sgu(hs, *sgu_args, ws_s, bias_s, tm_s, True)

    gfin = r1(final_norm)
    yp, cst_p1 = ffn(1, hp, y_p, w_sgu_out[0], gfin, False)
    ys, up_s1 = ffn(1, hs, y_s, w_sgu_out[0], gfin, True)

    tail = lambda c: c[:, SUBLANES - (CONV_W - 1):]
    stail = lambda u: u.reshape(Bs, Ts, 2 * D_FF)[:, Ts - (CONV_W - 1):]
    return (yp, ys.reshape(Bs, Ts, D_MODEL),
            k32.reshape(1, B, S, N_KV_HEADS, HEAD_DIM), v32.reshape(1, B, S, N_KV_HEADS, HEAD_DIM),
            kw[None, ..., :IDX_DIM],
            ks32.reshape(1, Bs, Ts, N_KV_HEADS, HEAD_DIM), vs32.reshape(1, Bs, Ts, N_KV_HEADS, HEAD_DIM),
            kws[..., :IDX_DIM].reshape(1, Bs, Ts, IDX_DIM),
            vn_s.reshape(1, Bs, Ts, D_MODEL),
            jnp.stack([tail(cst_p0), tail(cst_p1)]),
            jnp.stack([stail(up_s0), stail(up_s1)]))
```

```python
import functools

import numpy as np
import jax
import jax.numpy as jnp
from jax import lax
from jax.experimental import pallas as pl
from jax.experimental.pallas import tpu as pltpu

F32 = jnp.float32
BF16 = jnp.bfloat16

D_MODEL = 1024
N_HEADS = 8
HEAD_DIM = 128
N_KV_HEADS = 2
HEADS_PER_KV = N_HEADS // N_KV_HEADS
ROT_DIM = 32
ROPE_THETA = 500000.0
N_IDX_HEADS = 4
IDX_DIM = 64
IDX_ROT_DIM = 16
TOPK_MAX = 256
PAGE_SIZE = 128
Q_W = N_HEADS * HEAD_DIM
KV_W = N_KV_HEADS * HEAD_DIM
QI_W = N_IDX_HEADS * IDX_DIM
ATTN_IN = Q_W + 2 * KV_W + QI_W + IDX_DIM + N_IDX_HEADS
ATTN_IN_PAD = 1920
KW_COL = Q_W + 2 * KV_W + QI_W
CHUNK = 128
SGU_GROUPS = 4
SGU_GROUP_DIM = D_MODEL // SGU_GROUPS
D_FF = 2816
CONV_W = 3
NORM_EPS = 1e-6
LN_EPS = 1e-5
LOG2E = 1.4426950408889634
ATTN_SCALE_LOG2 = HEAD_DIM ** -0.5 * LOG2E
W_IDX_SCALE = (N_IDX_HEADS ** -0.5) * (IDX_DIM ** -0.5)

LANES = 128
SUBLANES = 8
KB = 128
SUPER = 512
ATTN_ROW_PARTS = 4
RANK_BLOCK = 256
FF_CHUNK = 256
PROMPT_ROWS = 1024
N_FF_CHUNKS = D_FF // FF_CHUNK
PAGES_PER_STEP = 32
IDX_PAGES_PER_STEP = 64
SAMPLE_CHAINS = 8
SEL_ROWS = 64
N_BISECT = 20
VMEM_LIMIT = 56 * 1024 * 1024
MASK_BIAS = -1e30

_NT = (((1,), (1,)), ((), ()))


def _cparams(n_axes):
    return pltpu.CompilerParams(
        dimension_semantics=("arbitrary",) * n_axes, vmem_limit_bytes=VMEM_LIMIT)


def _rmsnorm(x, g):
    return x * lax.rsqrt(jnp.mean(x * x, axis=-1, keepdims=True) + NORM_EPS) * g


def _rope(u, c, sa, sb, half):
    return u * c + pltpu.roll(u, LANES - half, 1) * sa + pltpu.roll(u, half, 1) * sb


def _attn_in_kernel(x_ref, g_ref, w_ref, ch_ref, sha_ref, shb_ref, ci_ref, sia_ref, sib_ref,
                    ckw_ref, skwa_ref, skwb_ref,
                    q_ref, k_ref, v_ref, kb_ref, vb_ref, qi_ref, kw_ref, kib_ref):
    xn = _rmsnorm(x_ref[...], g_ref[...]).astype(BF16)
    z = jnp.dot(xn, w_ref[...], preferred_element_type=F32)
    ch, sha, shb = ch_ref[...], sha_ref[...], shb_ref[...]
    for h in range(N_HEADS):
        sl = slice(h * HEAD_DIM, (h + 1) * HEAD_DIM)
        q_ref[:, sl] = (_rope(z[:, sl], ch, sha, shb, ROT_DIM // 2) * ATTN_SCALE_LOG2).astype(BF16)
    for h in range(N_KV_HEADS):
        sl = slice(h * HEAD_DIM, (h + 1) * HEAD_DIM)
        kk = _rope(z[:, Q_W + h * HEAD_DIM:Q_W + (h + 1) * HEAD_DIM], ch, sha, shb, ROT_DIM // 2)
        k_ref[pl.ds(h, kk.shape[0], stride=N_KV_HEADS), :] = kk
        kb_ref[:, sl] = kk.astype(BF16)
    v = z[:, Q_W + KV_W:Q_W + 2 * KV_W]
    for h in range(N_KV_HEADS):
        v_ref[pl.ds(h, v.shape[0], stride=N_KV_HEADS), :] = v[:, h * HEAD_DIM:(h + 1) * HEAD_DIM]
    vb_ref[...] = v.astype(BF16)
    ci, sia, sib = ci_ref[...], sia_ref[...], sib_ref[...]
    qi0 = Q_W + 2 * KV_W
    for s in range(QI_W // LANES):
        sl = slice(s * LANES, (s + 1) * LANES)
        qi_ref[:, sl] = _rope(z[:, qi0 + s * LANES:qi0 + (s + 1) * LANES], ci, sia, sib,
                              IDX_ROT_DIM // 2).astype(BF16)
    kw = _rope(z[:, KW_COL:KW_COL + LANES], ckw_ref[...], skwa_ref[...], skwb_ref[...],
               IDX_ROT_DIM // 2)
    kw_ref[...] = kw
    kib_ref[...] = kw[:, :IDX_DIM].astype(BF16)


def _rope_tables(pos, rot, period, tail_scale=None):
    half = rot // 2
    freqs = ROPE_THETA ** (-(jnp.arange(half, dtype=F32) * 2.0) / rot)
    ang = pos.astype(F32)[:, None] * freqs[None, :]
    cos, sin = jnp.cos(ang), jnp.sin(ang)
    lane = np.arange(LANES) % period
    col = lane % half
    c = jnp.where(lane < rot, cos[:, col], 1.0)
    sa = jnp.where(lane < half, -sin[:, col], 0.0)
    sb = jnp.where((lane >= half) & (lane < rot), sin[:, col], 0.0)
    if tail_scale is not None:
        glane = np.arange(LANES)
        c = jnp.where(glane < period, c, 1.0)
        c = jnp.where((glane >= IDX_DIM) & (glane < IDX_DIM + N_IDX_HEADS), tail_scale, c)
        sa = jnp.where(glane < period, sa, 0.0)
        sb = jnp.where(glane < period, sb, 0.0)
    return c.astype(F32), sa.astype(F32), sb.astype(F32)


def _attn_in(x, pos, g, w_pad, tm):
    B, T, _ = x.shape
    ch, sha, shb = _rope_tables(pos, ROT_DIM, HEAD_DIM)
    ci, sia, sib = _rope_tables(pos, IDX_ROT_DIM, IDX_DIM)
    ckw, skwa, skwb = _rope_tables(pos, IDX_ROT_DIM, IDX_DIM, tail_scale=W_IDX_SCALE)
    row = lambda b, t: (b, t, 0)
    tab = lambda b, t: (t, 0)
    const = lambda b, t: (0, 0)
    blk = lambda w: pl.BlockSpec((None, tm, w), row)
    tspec = pl.BlockSpec((tm, LANES), tab)
    kvrows = pl.BlockSpec((None, tm * N_KV_HEADS, HEAD_DIM), row)
    outs = pl.pallas_call(
        _attn_in_kernel,
        grid=(B, T // tm),
        in_specs=[blk(D_MODEL), pl.BlockSpec((1, D_MODEL), const),
                  pl.BlockSpec((D_MODEL, ATTN_IN_PAD), const),
                  tspec, tspec, tspec, tspec, tspec, tspec, tspec, tspec, tspec],
        out_specs=[blk(Q_W), kvrows, kvrows, blk(KV_W), blk(KV_W), blk(QI_W),
                   blk(LANES), blk(IDX_DIM)],
        out_shape=[jax.ShapeDtypeStruct((B, T, Q_W), BF16),
                   jax.ShapeDtypeStruct((B, T * N_KV_HEADS, HEAD_DIM), F32),
                   jax.ShapeDtypeStruct((B, T * N_KV_HEADS, HEAD_DIM), F32),
                   jax.ShapeDtypeStruct((B, T, KV_W), BF16),
                   jax.ShapeDtypeStruct((B, T, KV_W), BF16),
                   jax.ShapeDtypeStruct((B, T, QI_W), BF16),
                   jax.ShapeDtypeStruct((B, T, LANES), F32),
                   jax.ShapeDtypeStruct((B, T, IDX_DIM), BF16)],
        compiler_params=_cparams(2),
        name="attn_in",
    )(x, g, w_pad, ch, sha, shb, ci, sia, sib, ckw, skwa, skwb)
    return outs


def _search_threshold(count_ge, bracket, scan, lo0, hi0, n0, kf):
    hi0 = jnp.where(n0 <= kf, lo0, hi0)

    def coarse(_, st):
        lo, hi = st
        p = 0.5 * lo + 0.5 * hi
        ge = count_ge(p) >= kf
        return jnp.where(ge, p, lo), jnp.where(ge, hi, p)

    lo, hi = lax.fori_loop(0, N_BISECT, coarse, (lo0, hi0))
    a, b = bracket(lo, hi)

    def n_open(a, b):
        return jnp.max(jnp.where(a < b, 1, 0).astype(jnp.int32))

    def body(carry):
        _, a, b = carry
        p = jnp.minimum(jnp.maximum(0.5 * a + 0.5 * b, a), b)
        p = jnp.where(p > a, p, b)
        cnt, lowmax, highmin = scan(p)
        act = a < b
        a2 = jnp.where(act, jnp.where(cnt >= kf, highmin, a), a)
        b2 = jnp.where(act, jnp.where(cnt >= kf, b, lowmax), b)
        return n_open(a2, b2), a2, b2

    out = lax.while_loop(lambda c: c[0] > 0, body, (n_open(a, b), a, b))
    return out[1]


ACC_ROWS = 64


def _rows_reduce8(x, op):
    return op(x.reshape(x.shape[0] // ACC_ROWS, ACC_ROWS, LANES), axis=0)


def _attn_prompt_kernel(q_ref, kb_ref, vb_ref, qi_ref, kib_ref, wi_ref, o_ref,
                        sc_ref, bias_ref, vext_ref, tri_ref, m_ref, acc_ref, *, n_sel):
    i = pl.program_id(1)
    nsup = i // (SUPER // KB) + 1
    kf = float(n_sel)
    inf = jnp.float32(jnp.inf)

    @pl.when(i == 0)
    def _():
        for g in range(N_KV_HEADS):
            vext_ref[g, :, :HEAD_DIM] = vb_ref[:, g * HEAD_DIM:(g + 1) * HEAD_DIM]
            vext_ref[g, :, HEAD_DIM:] = jnp.ones((vext_ref.shape[1], HEAD_DIM), BF16)
        r = lax.broadcasted_iota(jnp.int32, (RANK_BLOCK, RANK_BLOCK), 0)
        c = lax.broadcasted_iota(jnp.int32, (RANK_BLOCK, RANK_BLOCK), 1)
        tri_ref[...] = jnp.where(c <= r, 1.0, 0.0).astype(BF16)

    def rows(u):
        return pl.ds(pl.multiple_of(u * SUPER, SUPER), SUPER)

    def sweep(fn, init):
        return lax.fori_loop(0, nsup, lambda u, c: fn(sc_ref[rows(u), :], c), init)

    z8 = jnp.zeros((ACC_ROWS, LANES), F32)
    fold = lambda acc8, op: op(acc8, axis=0, keepdims=True)

    w = wi_ref[...]
    qi = qi_ref[...]
    qi_rows = jnp.concatenate([qi[:, h * IDX_DIM:(h + 1) * IDX_DIM] for h in range(N_IDX_HEADS)],
                              axis=0)
    kmq = (lax.broadcasted_iota(jnp.int32, (SUPER, KB), 0)
           - lax.broadcasted_iota(jnp.int32, (SUPER, KB), 1))

    def p1(u, carry):
        amin, amax = carry
        kic = kib_ref[rows(u), :]
        sh = lax.dot_general(kic, qi_rows, _NT, preferred_element_type=F32)
        s = None
        for h in range(N_IDX_HEADS):
            t = w[h:h + 1, :] * jnp.maximum(sh[:, h * KB:(h + 1) * KB], 0.0)
            s = t if s is None else s + t
        causal = kmq <= i * KB - u * SUPER
        x = jnp.where(causal, s, -inf)
        sc_ref[rows(u), :] = x
        amin = jnp.minimum(amin, _rows_reduce8(jnp.where(causal, s, inf), jnp.min))
        amax = jnp.maximum(amax, _rows_reduce8(x, jnp.max))
        return amin, amax

    amin8, amax8 = lax.fori_loop(0, nsup, p1, (z8 + inf, z8 - inf))
    lane = lax.broadcasted_iota(jnp.int32, (1, LANES), 1)
    n0 = (i * KB + lane + 1).astype(F32)

    def count_ge(p):
        return fold(sweep(lambda x, c: c + _rows_reduce8(jnp.where(x >= p, 1.0, 0.0), jnp.sum), z8),
                    jnp.sum)

    def bracket(lo, hi):
        def f(x, c):
            return (jnp.minimum(c[0], _rows_reduce8(jnp.where(x >= lo, x, inf), jnp.min)),
                    jnp.maximum(c[1], _rows_reduce8(jnp.where(x <= hi, x, -inf), jnp.max)))
        a8, b8 = sweep(f, (z8 + inf, z8 - inf))
        return fold(a8, jnp.min), fold(b8, jnp.max)

    def scan(p):
        def f(x, c):
            ge = x >= p
            return (c[0] + _rows_reduce8(jnp.where(ge, 1.0, 0.0), jnp.sum),
                    jnp.maximum(c[1], _rows_reduce8(jnp.where(ge, -inf, x), jnp.max)),
                    jnp.minimum(c[2], _rows_reduce8(jnp.where(ge, x, inf), jnp.min)))
        cnt, lo, hi = sweep(f, (z8, z8 - inf, z8 + inf))
        return fold(cnt, jnp.sum), fold(lo, jnp.max), fold(hi, jnp.min)

    lo0 = fold(amin8, jnp.min)
    thr = lax.cond((i + 1) * KB <= n_sel, lambda: lo0,
                   lambda: _search_threshold(count_ge, bracket, scan, lo0, fold(amax8, jnp.max),
                                             n0, kf))

    cgt = fold(sweep(lambda x, c: c + _rows_reduce8(jnp.where(x > thr, 1.0, 0.0), jnp.sum), z8),
               jnp.sum)
    need = kf - cgt

    tri = tri_ref[...]

    def p3(u, run):
        for part in range(SUPER // RANK_BLOCK):
            rs = pl.ds(pl.multiple_of(u * SUPER + part * RANK_BLOCK, RANK_BLOCK), RANK_BLOCK)
            x = sc_ref[rs, :]
            eq = x == thr
            rank = jnp.dot(tri, jnp.where(eq, 1.0, 0.0).astype(BF16),
                           preferred_element_type=F32) + run
            keep_tie = jnp.where(rank <= need, 0.0, MASK_BIAS)
            bias = jnp.where(x > thr, 0.0, jnp.where(eq, keep_tie, MASK_BIAS))
            bias_ref[rs, :] = bias.astype(BF16)
            run = rank[RANK_BLOCK - 1:RANK_BLOCK, :]
        return run

    lax.fori_loop(0, nsup, p3, jnp.zeros((1, LANES), F32))

    eye = jnp.where(lax.broadcasted_iota(jnp.int32, (KB, KB), 0)
                    == lax.broadcasted_iota(jnp.int32, (KB, KB), 1), 1.0, 0.0).astype(BF16)
    eye4 = jnp.concatenate([eye] * HEADS_PER_KV, axis=0)
    ags = []
    for g in range(N_KV_HEADS):
        qg = jnp.concatenate(
            [q_ref[:, (g * HEADS_PER_KV + r) * HEAD_DIM:(g * HEADS_PER_KV + r + 1) * HEAD_DIM]
             for r in range(HEADS_PER_KV)], axis=0)
        ags.append(jnp.concatenate([qg, eye4], axis=1))
    m_ref[...] = jnp.full(m_ref.shape, -inf, F32)
    acc_ref[...] = jnp.zeros(acc_ref.shape, F32)
    nrow = HEADS_PER_KV * KB
    part = nrow // ATTN_ROW_PARTS

    def qk(u):
        bias_t = bias_ref[rows(u), :]
        out = []
        for g in range(N_KV_HEADS):
            bmat = jnp.concatenate([kb_ref[rows(u), g * HEAD_DIM:(g + 1) * HEAD_DIM], bias_t],
                                   axis=1)
            out.append(lax.dot_general(ags[g], bmat, _NT,
                                       preferred_element_type=F32).astype(BF16))
        return tuple(out)

    def softmax_pv(u, ts, nkeys=SUPER):
        for g in range(N_KV_HEADS):
            vx = vext_ref[g, pl.ds(pl.multiple_of(u * SUPER, SUPER), nkeys), :]
            for hp in range(ATTN_ROW_PARTS):
                rs = slice(hp * part, (hp + 1) * part)
                t = ts[g][rs, :nkeys]
                m_old = m_ref[g, rs]
                m_new = jnp.maximum(m_old, jnp.max(t, axis=1, keepdims=True).astype(F32))
                p = jnp.exp2(t - m_new.astype(BF16))
                acc_ref[g, rs] = (jnp.exp2(m_old - m_new) * acc_ref[g, rs]
                                  + jnp.dot(p, vx, preferred_element_type=F32))
                m_ref[g, rs] = m_new

    def pa(u, ts):
        nxt = qk(u + 1)
        softmax_pv(u, ts)
        return nxt

    ts_last = lax.fori_loop(0, nsup - 1, pa, qk(0))
    blocks_per_super = SUPER // KB
    short = i % blocks_per_super < blocks_per_super // 2

    @pl.when(short)
    def _():
        softmax_pv(nsup - 1, ts_last, SUPER // 2)

    @pl.when(jnp.logical_not(short))
    def _():
        softmax_pv(nsup - 1, ts_last)

    for g in range(N_KV_HEADS):
        acc = acc_ref[g]
        o = acc[:, :HEAD_DIM] / acc[:, HEAD_DIM:]
        for r in range(HEADS_PER_KV):
            h = g * HEADS_PER_KV + r
            o_ref[:, h * HEAD_DIM:(h + 1) * HEAD_DIM] = o[r * KB:(r + 1) * KB].astype(BF16)


def _attn_prompt(q, kb, vb, qi, kib, wi_t):
    B, S, _ = q.shape
    n_sel = min(TOPK_MAX, S // 4)
    qblk = lambda w: pl.BlockSpec((None, KB, w), lambda b, i: (b, i, 0))
    full = lambda w: pl.BlockSpec((None, S, w), lambda b, i: (b, 0, 0))
    nrow = HEADS_PER_KV * KB
    return pl.pallas_call(
        functools.partial(_attn_prompt_kernel, n_sel=n_sel),
        grid=(B, S // KB),
        in_specs=[qblk(Q_W), full(KV_W), full(KV_W), qblk(QI_W), full(IDX_DIM),
                  pl.BlockSpec((None, N_IDX_HEADS, KB), lambda b, i: (b, 0, i))],
        out_specs=qblk(Q_W),
        out_shape=jax.ShapeDtypeStruct((B, S, Q_W), BF16),
        scratch_shapes=[pltpu.VMEM((S, KB), F32), pltpu.VMEM((S, KB), BF16),
                        pltpu.VMEM((N_KV_HEADS, S, 2 * HEAD_DIM), BF16),
                        pltpu.VMEM((RANK_BLOCK, RANK_BLOCK), BF16),
                        pltpu.VMEM((N_KV_HEADS, nrow, 1), F32),
                        pltpu.VMEM((N_KV_HEADS, nrow, 2 * HEAD_DIM), F32)],
        compiler_params=_cparams(2),
        name="attn_prompt",
    )(q, kb, vb, qi, kib, wi_t)


def _idx_scores(qi, wcol, keys_t):
    s = jnp.dot(qi, keys_t, preferred_element_type=F32)
    r = jnp.maximum(s, 0.0) * wcol
    out = r[0:SUBLANES]
    for h in range(1, N_IDX_HEADS):
        out = out + r[h * SUBLANES:(h + 1) * SUBLANES]
    return out


def _score_sample_kernel(pt_ref, qi_ref, w_ref, *refs, past):
    ki_refs = refs[:IDX_PAGES_PER_STEP]
    kinew_ref, sc_ref = refs[IDX_PAGES_PER_STEP:]
    j = pl.program_id(1)
    qi = qi_ref[...]
    wcol = w_ref[:, 0:1]
    width = IDX_PAGES_PER_STEP * PAGE_SIZE
    keys_t = jnp.concatenate([r[...] for r in ki_refs], axis=1).astype(BF16)
    nq = sc_ref.shape[0]
    sc_ref[:, pl.ds(pl.multiple_of(j * width, width), width)] = _idx_scores(qi, wcol, keys_t)[:nq]

    @pl.when(j == pl.num_programs(1) - 1)
    def _():
        inf = jnp.float32(jnp.inf)
        row = lax.broadcasted_iota(jnp.int32, (SUBLANES, PAGE_SIZE), 0)
        lane = lax.broadcasted_iota(jnp.int32, (SUBLANES, PAGE_SIZE), 1)
        snew = _idx_scores(qi, wcol, kinew_ref[...])
        sc_ref[:, past:past + PAGE_SIZE] = jnp.where(lane <= row, snew, -inf)[:nq]
        sc_ref[:, past + PAGE_SIZE:] = jnp.full((nq, SUPER - PAGE_SIZE), -inf, F32)


def _score_sample(page_table, qi_pad, wcol, cache_ki_t, kinew_t, nq):
    B, n_pages = page_table.shape
    past = n_pages * PAGE_SIZE
    steps = n_pages // IDX_PAGES_PER_STEP
    page = lambda p: pl.BlockSpec((None, IDX_DIM, PAGE_SIZE),
                                  lambda b, j, pt: (pt[b, j * IDX_PAGES_PER_STEP + p], 0, 0))
    per_b = lambda r, w: pl.BlockSpec((None, r, w), lambda b, j, pt: (b, 0, 0))
    return pl.pallas_call(
        functools.partial(_score_sample_kernel, past=past),
        grid_spec=pltpu.PrefetchScalarGridSpec(
            num_scalar_prefetch=1,
            grid=(B, steps),
            in_specs=[per_b(N_IDX_HEADS * SUBLANES, IDX_DIM), per_b(N_IDX_HEADS * SUBLANES, LANES)]
                     + [page(p) for p in range(IDX_PAGES_PER_STEP)]
                     + [per_b(IDX_DIM, PAGE_SIZE)],
            out_specs=per_b(nq, past + SUPER),
        ),
        out_shape=jax.ShapeDtypeStruct((B, nq, past + SUPER), F32),
        compiler_params=_cparams(2),
        name="score_sample",
    )(page_table, qi_pad, wcol, *([cache_ki_t] * IDX_PAGES_PER_STEP), kinew_t)


def _lanes_reduce(x, op):
    out = x[:, :LANES]
    for t in range(1, x.shape[1] // LANES):
        out = op(out, x[:, t * LANES:(t + 1) * LANES])
    return out


def _select_sample_kernel(sc_ref, sel_ref, tri_ref, spread_ref, *, n_sel):
    R, L = sc_ref.shape
    nsup = L // SUPER
    kf = float(n_sel)
    inf = jnp.float32(jnp.inf)
    r = lax.broadcasted_iota(jnp.int32, (SUPER, SUPER), 0)
    c = lax.broadcasted_iota(jnp.int32, (SUPER, SUPER), 1)
    tri_ref[...] = jnp.where(r <= c, 1.0, 0.0).astype(BF16)

    def cols(u):
        return pl.ds(pl.multiple_of(u * SUPER, SUPER), SUPER)

    def sweep(fn, init):
        return lax.fori_loop(0, nsup, lambda u, cr: fn(sc_ref[:, cols(u)], cr), init)

    zr = jnp.zeros((R, LANES), F32)
    rsum = lambda a: jnp.sum(a, axis=1, keepdims=True)
    rmax = lambda a: jnp.max(a, axis=1, keepdims=True)
    rmin = lambda a: jnp.min(a, axis=1, keepdims=True)
    one = lambda m: jnp.where(m, 1.0, 0.0)

    def f0(x, cr):
        fin = x > -inf
        return (jnp.minimum(cr[0], _lanes_reduce(jnp.where(fin, x, inf), jnp.minimum)),
                jnp.maximum(cr[1], _lanes_reduce(x, jnp.maximum)),
                cr[2] + _lanes_reduce(one(fin), jnp.add))
    lo8, hi8, n8 = sweep(f0, (zr + inf, zr - inf, zr))

    def count_ge(p):
        return rsum(sweep(lambda x, cr: cr + _lanes_reduce(one(x >= p), jnp.add), zr))

    def bracket(lo, hi):
        def f(x, cr):
            return (jnp.minimum(cr[0], _lanes_reduce(jnp.where(x >= lo, x, inf), jnp.minimum)),
                    jnp.maximum(cr[1], _lanes_reduce(jnp.where(x <= hi, x, -inf), jnp.maximum)))
        a8, b8 = sweep(f, (zr + inf, zr - inf))
        return rmin(a8), rmax(b8)

    def scan(p):
        def f(x, cr):
            ge = x >= p
            return (cr[0] + _lanes_reduce(one(ge), jnp.add),
                    jnp.maximum(cr[1], _lanes_reduce(jnp.where(ge, -inf, x), jnp.maximum)),
                    jnp.minimum(cr[2], _lanes_reduce(jnp.where(ge, x, inf), jnp.minimum)))
        cnt, lo, hi = sweep(f, (zr, zr - inf, zr + inf))
        return rsum(cnt), rmax(lo), rmin(hi)

    thr = _search_threshold(count_ge, bracket, scan, rmin(lo8), rmax(hi8), rsum(n8), kf)
    need = kf - rsum(sweep(lambda x, cr: cr + _lanes_reduce(one(x > thr), jnp.add), zr))

    wide = N_KV_HEADS * SUPER
    spread_ref[...] = jnp.where(
        lax.broadcasted_iota(jnp.int32, (SUPER, wide), 1) // N_KV_HEADS
        == lax.broadcasted_iota(jnp.int32, (SUPER, wide), 0), 1.0, 0.0).astype(BF16)

    def p3(u, run):
        x = sc_ref[:, cols(u)]
        eq = x == thr
        rank = jnp.dot(one(eq).astype(BF16), tri_ref[...], preferred_element_type=F32) + run
        sel = jnp.where(x > thr, 1.0, jnp.where(eq, one(rank <= need), 0.0))
        sel_ref[:, pl.ds(pl.multiple_of(u * wide, wide), wide)] = jnp.dot(
            sel.astype(BF16), spread_ref[...], preferred_element_type=F32)
        return rank[:, SUPER - 1:SUPER]

    lax.fori_loop(0, nsup, p3, jnp.zeros((R, 1), F32))


def _select_sample(sc, n_sel):
    R, L = sc.shape
    return pl.pallas_call(
        functools.partial(_select_sample_kernel, n_sel=n_sel),
        grid=(R // SEL_ROWS,),
        in_specs=[pl.BlockSpec((SEL_ROWS, L), lambda i: (i, 0))],
        out_specs=pl.BlockSpec((SEL_ROWS, N_KV_HEADS * L), lambda i: (i, 0)),
        out_shape=jax.ShapeDtypeStruct((R, N_KV_HEADS * L), F32),
        scratch_shapes=[pltpu.VMEM((SUPER, SUPER), BF16),
                        pltpu.VMEM((SUPER, N_KV_HEADS * SUPER), BF16)],
        compiler_params=_cparams(1),
        name="select_sample",
    )(sc)


def _attn_sample_kernel(pt_ref, q_ref, sel_ref, *refs, past):
    k_refs = refs[:PAGES_PER_STEP]
    v_refs = refs[PAGES_PER_STEP:2 * PAGES_PER_STEP]
    knew_ref, vnew_ref, o_ref, m_ref, acc_ref = refs[2 * PAGES_PER_STEP:]
    j = pl.program_id(1)
    nq = sel_ref.shape[0]
    nrow = nq * N_HEADS
    pw = N_KV_HEADS * PAGE_SIZE
    per_chain = PAGES_PER_STEP // SAMPLE_CHAINS

    @pl.when(j == 0)
    def _():
        m_ref[...] = jnp.full(m_ref.shape, -jnp.inf, F32)
        acc_ref[...] = jnp.zeros(acc_ref.shape, F32)

    q = q_ref[...]

    def update(ch, kk, vv, col):
        n = kk.shape[0]
        s = lax.dot_general(q, kk, _NT, preferred_element_type=F32)
        sel = sel_ref[:, pl.ds(col, n)]
        sel = jnp.concatenate([jnp.broadcast_to(sel[t:t + 1], (N_HEADS, n)) for t in range(nq)],
                              axis=0)
        own = (lax.broadcasted_iota(jnp.int32, (nrow, n), 0) % N_HEADS // HEADS_PER_KV
               == lax.broadcasted_iota(jnp.int32, (nrow, n), 1) % N_KV_HEADS)
        s = jnp.where(own, jnp.where(sel > 0.5, s, MASK_BIAS), MASK_BIAS)
        m_old = m_ref[ch]
        m_new = jnp.maximum(m_old, jnp.max(s, axis=1, keepdims=True))
        pb = jnp.exp2(s - m_new).astype(BF16)
        vext = jnp.concatenate([vv, jnp.ones((n, HEAD_DIM), BF16)], axis=1)
        acc_ref[ch] = (jnp.exp2(m_old - m_new) * acc_ref[ch]
                       + jnp.dot(pb, vext, preferred_element_type=F32))
        m_ref[ch] = m_new

    for ch in range(SAMPLE_CHAINS):
        pages = range(ch * per_chain, (ch + 1) * per_chain)
        kk = jnp.concatenate([k_refs[p][...].astype(BF16) for p in pages], axis=0)
        vv = jnp.concatenate([v_refs[p][...].astype(BF16) for p in pages], axis=0)
        col = pl.multiple_of((j * PAGES_PER_STEP + ch * per_chain) * pw, per_chain * pw)
        update(ch, kk, vv, col)

    @pl.when(j == pl.num_programs(1) - 1)
    def _():
        update(0, knew_ref[...], vnew_ref[...], N_KV_HEADS * past)
        m = m_ref[0]
        for ch in range(1, SAMPLE_CHAINS):
            m = jnp.maximum(m, m_ref[ch])
        acc = jnp.exp2(m_ref[0] - m) * acc_ref[0]
        for ch in range(1, SAMPLE_CHAINS):
            acc = acc + jnp.exp2(m_ref[ch] - m) * acc_ref[ch]
        o_ref[...] = acc[:, :HEAD_DIM] / acc[:, HEAD_DIM:]


def _attn_sample(page_table, q_all, sel, cache_k, cache_v, knew, vnew):
    B, n_pages = page_table.shape
    past = n_pages * PAGE_SIZE
    steps = n_pages // PAGES_PER_STEP
    nq = sel.shape[1]
    nrow = nq * N_HEADS
    pw = N_KV_HEADS * PAGE_SIZE
    page = lambda p: pl.BlockSpec((pw, HEAD_DIM), lambda b, j, pt: (pt[b, j * PAGES_PER_STEP + p], 0))
    per_b = lambda r, w: pl.BlockSpec((None, r, w), lambda b, j, pt: (b, 0, 0))
    return pl.pallas_call(
        functools.partial(_attn_sample_kernel, past=past),
        grid_spec=pltpu.PrefetchScalarGridSpec(
            num_scalar_prefetch=1,
            grid=(B, steps),
            in_specs=[per_b(nrow, HEAD_DIM), per_b(nq, sel.shape[2])]
                     + [page(p) for p in range(PAGES_PER_STEP)] * 2
                     + [per_b(pw, HEAD_DIM), per_b(pw, HEAD_DIM)],
            out_specs=per_b(nrow, HEAD_DIM),
            scratch_shapes=[pltpu.VMEM((SAMPLE_CHAINS, nrow, 1), F32),
                            pltpu.VMEM((SAMPLE_CHAINS, nrow, 2 * HEAD_DIM), F32)],
        ),
        out_shape=jax.ShapeDtypeStruct((B, nrow, HEAD_DIM), F32),
        compiler_params=_cparams(2),
        name="attn_sample",
    )(page_table, q_all, sel, *([cache_k] * PAGES_PER_STEP), *([cache_v] * PAGES_PER_STEP), knew, vnew)


def _ffn_kernel(*refs, sample, final):
    it = iter(refs)
    h_ref, o_ref, wout_ref, g_ref, wup_ref, cw_ref, cb_ref, wdown_ref = (next(it) for _ in range(8))
    gfin_ref = next(it) if final else None
    st1_ref, st2_ref = (next(it), next(it)) if sample else (None, None)
    hout_ref, cst_ref = next(it), next(it)
    act_ref = next(it)
    carry_ref = None if sample else next(it)

    tm = h_ref.shape[0]
    h1 = h_ref[...] + jnp.dot(o_ref[...], wout_ref[...], preferred_element_type=F32)
    xn = _rmsnorm(h1, g_ref[...]).astype(BF16)

    if sample:
        t4 = lax.broadcasted_iota(jnp.int32, (tm, FF_CHUNK), 0) % 4
    else:
        @pl.when(pl.program_id(1) == 0)
        def _():
            carry_ref[...] = jnp.zeros(carry_ref.shape, F32)
        row8 = lax.broadcasted_iota(jnp.int32, (SUBLANES, FF_CHUNK), 0)

    def conv(cols):
        u = jnp.dot(xn, wup_ref[:, cols], preferred_element_type=F32)
        r1 = pltpu.roll(u, 1, 0)
        r2 = pltpu.roll(u, 2, 0)
        if sample:
            s1 = jnp.where(t4 == 0, st1_ref[:, cols], r1)
            s2 = jnp.where(t4 == 0, st2_ref[:, cols], jnp.where(t4 == 1, st1_ref[:, cols], r2))
            cst_ref[:, cols] = u
        else:
            prev = carry_ref[:, cols]
            top1 = jnp.where(row8 < 1, pltpu.roll(prev, 1, 0), r1[:SUBLANES])
            top2 = jnp.where(row8 < 2, pltpu.roll(prev, 2, 0), r2[:SUBLANES])
            s1 = jnp.concatenate([top1, r1[SUBLANES:]], axis=0)
            s2 = jnp.concatenate([top2, r2[SUBLANES:]], axis=0)
            carry_ref[:, cols] = u[tm - SUBLANES:]
            cst_ref[:, cols] = u[tm - SUBLANES:]
        cw = cw_ref[:, cols]
        return cb_ref[:, cols] + cw[0:1] * s2 + cw[1:2] * s1 + cw[2:3] * u

    for c in range(N_FF_CHUNKS):
        gate = conv(slice(c * FF_CHUNK, (c + 1) * FF_CHUNK))
        val = conv(slice(D_FF + c * FF_CHUNK, D_FF + (c + 1) * FF_CHUNK))
        act_ref[:, c * FF_CHUNK:(c + 1) * FF_CHUNK] = (
            gate * (1.0 / (1.0 + jnp.exp(-gate))) * val).astype(BF16)

    out = h1 + jnp.dot(act_ref[...], wdown_ref[...], preferred_element_type=F32)
    if final:
        out = _rmsnorm(out, gfin_ref[...])
    hout_ref[...] = out


def _ffn(h, o, wout, g, wup, cw, cb, wdown, gfin, state, tm):
    B, T, _ = h.shape
    sample = state is not None
    final = gfin is not None
    row = lambda b, t: (b, t, 0)
    const = lambda b, t: (0, 0)
    w = lambda a: pl.BlockSpec(a.shape, const, pipeline_mode=pl.Buffered(1))
    operands = [h, o, wout, g, wup, cw, cb, wdown]
    in_specs = [pl.BlockSpec((None, tm, D_MODEL), row), pl.BlockSpec((None, tm, D_MODEL), row),
                w(wout), w(g), w(wup), w(cw), w(cb), w(wdown)]
    if final:
        operands.append(gfin)
        in_specs.append(w(gfin))
    scratch = [pltpu.VMEM((tm, D_FF), BF16)]
    if sample:
        operands += [state[0], state[1]]
        in_specs += [w(state[0]), w(state[1])]
        cst_shape = jax.ShapeDtypeStruct((B, T, 2 * D_FF), F32)
        cst_spec = pl.BlockSpec((None, tm, 2 * D_FF), row)
    else:
        cst_shape = jax.ShapeDtypeStruct((B, SUBLANES, 2 * D_FF), F32)
        cst_spec = pl.BlockSpec((None, SUBLANES, 2 * D_FF), lambda b, t: (b, 0, 0))
        scratch.append(pltpu.VMEM((SUBLANES, 2 * D_FF), F32))
    return pl.pallas_call(
        functools.partial(_ffn_kernel, sample=sample, final=final),
        grid=(B, T // tm),
        in_specs=in_specs,
        out_specs=[pl.BlockSpec((None, tm, D_MODEL), row), cst_spec],
        out_shape=[jax.ShapeDtypeStruct((B, T, D_MODEL), F32), cst_shape],
        scratch_shapes=scratch,
        compiler_params=_cparams(2),
        name="ffn_sample" if sample else "ffn_prompt",
    )(*operands)


def _sgu_kernel(h_ref, g_ref, win_ref, lng_ref, lnb_ref, ws_ref, bias_ref, *out_refs, emit_v):
    y_ref = out_refs[0]
    tm = h_ref.shape[0]
    xn = _rmsnorm(h_ref[...], g_ref[...]).astype(BF16)
    uv = jnp.dot(xn, win_ref[...], preferred_element_type=F32)
    u = uv[:, :D_MODEL]
    v = uv[:, D_MODEL:]
    xc = v - jnp.mean(v, axis=-1, keepdims=True)
    vn = xc * lax.rsqrt(jnp.mean(xc * xc, axis=-1, keepdims=True) + LN_EPS) * lng_ref[...] + lnb_ref[...]
    if emit_v:
        out_refs[1][...] = vn
    vnb = vn.astype(BF16)
    tri = (lax.broadcasted_iota(jnp.int32, (CHUNK, CHUNK), 0)
           >= lax.broadcasted_iota(jnp.int32, (CHUNK, CHUNK), 1))
    for g in range(SGU_GROUPS):
        wg = jnp.where(tri, ws_ref[g], 0.0).astype(BF16)
        cols = slice(g * SGU_GROUP_DIM, (g + 1) * SGU_GROUP_DIM)
        for c in range(tm // CHUNK):
            rows = slice(c * CHUNK, (c + 1) * CHUNK)
            s = jnp.dot(wg, vnb[rows, cols], preferred_element_type=F32) + bias_ref[:, cols]
            y_ref[rows, cols] = (u[rows, cols] * s).astype(BF16)


def _sgu_bias(bs):
    return jnp.repeat(bs.T, SGU_GROUP_DIM, axis=1)


def _sgu_sample_weights(ws, bs, B, T):
    corner = ws[:, :T, :T]
    ws_s = jnp.einsum('bc,gts->gbtcs', jnp.eye(B, dtype=ws.dtype), corner)
    return ws_s.reshape(SGU_GROUPS, B * T, B * T), jnp.tile(_sgu_bias(bs)[:T], (B, 1))


def _sgu(h, g, win, lng, lnb, ws, bias, tm, emit_v):
    B, T, _ = h.shape
    row = lambda b, t: (b, t, 0)
    w = lambda a: pl.BlockSpec(a.shape, lambda b, t: (0,) * a.ndim)
    out_shape = [jax.ShapeDtypeStruct((B, T, D_MODEL), BF16)]
    out_specs = [pl.BlockSpec((None, tm, D_MODEL), row)]
    if emit_v:
        out_shape.append(jax.ShapeDtypeStruct((B, T, D_MODEL), F32))
        out_specs.append(pl.BlockSpec((None, tm, D_MODEL), row))
    return pl.pallas_call(
        functools.partial(_sgu_kernel, emit_v=emit_v),
        grid=(B, T // tm),
        in_specs=[pl.BlockSpec((None, tm, D_MODEL), row), w(g), w(win), w(lng), w(lnb), w(ws), w(bias)],
        out_specs=out_specs,
        out_shape=out_shape,
        compiler_params=_cparams(2),
        name="sgu",
    )(h, g, win, lng, lnb, ws, bias)


def _pad_axis(a, axis, size):
    pad = [(0, 0)] * a.ndim
    pad[axis] = (0, size - a.shape[axis])
    return jnp.pad(a, pad)


def _sample_attention(qs, ksb, vsb, qis, kws, kisb, page_table, cache_k, cache_v, cache_ki, Bs, Ts):
    Ns = Bs * Ts
    n_pool = cache_k.shape[1]
    past = page_table.shape[1] * PAGE_SIZE
    qi_pad = _pad_axis(qis.reshape(Bs, Ts, N_IDX_HEADS, IDX_DIM).transpose(0, 2, 1, 3), 2, SUBLANES)
    qi_pad = qi_pad.reshape(Bs, N_IDX_HEADS * SUBLANES, IDX_DIM)
    wcol = kws[0, :, IDX_DIM:IDX_DIM + N_IDX_HEADS].reshape(Bs, Ts, N_IDX_HEADS).transpose(0, 2, 1)
    wcol = _pad_axis(wcol, 2, SUBLANES).reshape(Bs, N_IDX_HEADS * SUBLANES, 1)
    wcol = jnp.broadcast_to(wcol, (Bs, N_IDX_HEADS * SUBLANES, LANES))
    kinew_t = jnp.swapaxes(_pad_axis(kisb.reshape(Bs, Ts, IDX_DIM), 1, PAGE_SIZE), 1, 2)
    cache_ki_t = jnp.swapaxes(cache_ki, 2, 3).reshape(n_pool, IDX_DIM, PAGE_SIZE)
    sc = _score_sample(page_table, qi_pad, wcol, cache_ki_t, kinew_t, Ts)
    n_sel = min(TOPK_MAX, (past + Ts) // 4)
    sel = _select_sample(sc.reshape(Ns, past + SUPER), n_sel)
    sel = sel.reshape(Bs, Ts, N_KV_HEADS * (past + SUPER))
    q_all = qs.reshape(Bs, Ts * N_HEADS, HEAD_DIM)
    pw = N_KV_HEADS * PAGE_SIZE
    knew = _pad_axis(ksb.reshape(Bs, Ts * N_KV_HEADS, HEAD_DIM), 1, pw)
    vnew = _pad_axis(vsb.reshape(Bs, Ts * N_KV_HEADS, HEAD_DIM), 1, pw)
    o_s = _attn_sample(page_table, q_all, sel, cache_k.reshape(n_pool * pw, HEAD_DIM),
                       cache_v.reshape(n_pool * pw, HEAD_DIM), knew, vnew)
    return o_s.reshape(1, Ns, Q_W).astype(BF16)


def kernel(x_prompt, x_sample, cache_k, cache_v, cache_idx_k, state_ffn_conv, page_table, attn_norm,
           w_attn_in, w_attn_out, sgu_norm, w_sgu_in, sgu_ln_g, sgu_ln_b, sgu_ws, sgu_bs, w_sgu_out,
           ffn_norm, w_ffn_up, ffn_conv_w, ffn_conv_b, w_ffn_down, final_norm):
    B, S, _ = x_prompt.shape
    Bs, Ts, _ = x_sample.shape
    Ns = Bs * Ts
    n_pages = page_table.shape[1]
    past = n_pages * PAGE_SIZE
    assert w_ffn_up.shape[0] == 2 and w_attn_in.shape[0] == 1 and w_sgu_in.shape[0] == 1
    assert Ns == CHUNK and Ts == 4 and S % SUPER == 0
    assert n_pages % PAGES_PER_STEP == 0 and n_pages % IDX_PAGES_PER_STEP == 0 and Ns % SEL_ROWS == 0
    bf = lambda a: a.astype(BF16)
    r1 = lambda a: a[None]
    tm_p, tm_s = (PROMPT_ROWS if S % PROMPT_ROWS == 0 else SUPER), Ns

    w_in = bf(_pad_axis(w_attn_in[0], 1, ATTN_IN_PAD))
    g_attn = r1(attn_norm[0])
    q, k32, v32, kb, vb, qi, kw, kib = _attn_in(x_prompt, jnp.arange(S), g_attn, w_in, tm_p)
    wi_t = jnp.swapaxes(kw[..., IDX_DIM:IDX_DIM + N_IDX_HEADS], 1, 2)
    o_p = _attn_prompt(q, kb, vb, qi, kib, wi_t)

    xs = x_sample.reshape(1, Ns, D_MODEL)
    pos_s = past + (jnp.arange(Ns) % Ts)
    qs, ks32, vs32, ksb, vsb, qis, kws, kisb = _attn_in(xs, pos_s, g_attn, w_in, tm_s)
    o_s = _sample_attention(qs, ksb, vsb, qis, kws, kisb, page_table,
                            cache_k[0:1], cache_v[0:1], cache_idx_k[0:1], Bs, Ts)

    def ffn(i, h, o, wout, gfin, sample):
        state = None
        if sample:
            st = state_ffn_conv[i]
            state = (jnp.repeat(st[:, 1], Ts, axis=0), jnp.repeat(st[:, 0], Ts, axis=0))
        return _ffn(h, o, bf(wout), r1(ffn_norm[i]), bf(w_ffn_up[i]), ffn_conv_w[i],
                    r1(ffn_conv_b[i]), bf(w_ffn_down[i]), gfin, state, tm_s if sample else tm_p)

    hp, cst_p0 = ffn(0, x_prompt, o_p, w_attn_out[0], None, False)
    hs, up_s0 = ffn(0, xs, o_s, w_attn_out[0], None, True)

    sgu_args = (r1(sgu_norm[0]), bf(w_sgu_in[0]), r1(sgu_ln_g[0]), r1(sgu_ln_b[0]))
    (y_p,) = _sgu(hp, *sgu_args, sgu_ws[0], _sgu_bias(sgu_bs[0]), tm_p, False)
    ws_s, bias_s = _sgu_sample_weights(sgu_ws[0], sgu_bs[0], Bs, Ts)
    y_s, vn_s = _sgu(hs, *sgu_args, ws_s, bias_s, tm_s, True)

    gfin = r1(final_norm)
    yp, cst_p1 = ffn(1, hp, y_p, w_sgu_out[0], gfin, False)
    ys, up_s1 = ffn(1, hs, y_s, w_sgu_out[0], gfin, True)

    tail = lambda c: c[:, SUBLANES - (CONV_W - 1):]
    stail = lambda u: u.reshape(Bs, Ts, 2 * D_FF)[:, Ts - (CONV_W - 1):]
    return (yp, ys.reshape(Bs, Ts, D_MODEL),
            k32.reshape(1, B, S, N_KV_HEADS, HEAD_DIM), v32.reshape(1, B, S, N_KV_HEADS, HEAD_DIM),
            kw[None, ..., :IDX_DIM],
            ks32.reshape(1, Bs, Ts, N_KV_HEADS, HEAD_DIM), vs32.reshape(1, Bs, Ts, N_KV_HEADS, HEAD_DIM),
            kws[..., :IDX_DIM].reshape(1, Bs, Ts, IDX_DIM),
            vn_s.reshape(1, Bs, Ts, D_MODEL),
            jnp.stack([tail(cst_p0), tail(cst_p1)]),
            jnp.stack([stail(up_s0), stail(up_s1)]))
```
